```python
import math
import jax, jax.numpy as jnp
from jax import lax
import numpy as np

D_MODEL = 1024
BATCH = 8
SEQ = 2048
DEPTH = 1
DEC_BATCH = 16
DEC_SEQ = 2048
PAST_LEN = 128

N_META = 16
SSM_HEADS = 16
SSM_HEAD_DIM = 64
D_SSM = SSM_HEADS * SSM_HEAD_DIM
SSM_GROUPS = 2
HEADS_PER_GROUP = SSM_HEADS // SSM_GROUPS
D_STATE = 128
D_CONV = 5
D_CONV_CH = D_SSM + 2 * SSM_GROUPS * D_STATE
CHUNK = 128
META_PAD = CHUNK - N_META
MLA_HEADS = 8
QK_NOPE_DIM = 64
QK_ROPE_DIM = 32
V_HEAD_DIM = 64
Q_LORA_RANK = 384
KV_LORA_RANK = 256
D_ATTN = MLA_HEADS * V_HEAD_DIM
ROPE_THETA = 10000.0
Q_BLOCK = 128
D_MIX = D_SSM + D_ATTN
SPLIT_Z = D_SSM
SPLIT_XBC = SPLIT_Z + D_CONV_CH
SPLIT_DT = SPLIT_XBC + 2 * SSM_HEADS
SPLIT_CQ = SPLIT_DT + Q_LORA_RANK
SPLIT_CKV = SPLIT_CQ + KV_LORA_RANK
D_IN_PROJ = SPLIT_CKV + QK_ROPE_DIM
PEER_HEADS = 8
N_KEYS = 128
N_EXPERTS = N_KEYS * N_KEYS
PEER_TOPK = 16
D_KEY = 256
D_SUBKEY = D_KEY // 2
PEER_BLOCK = 256
DEEPNORM_ALPHA = (2.0 * DEPTH) ** 0.25
DEEPNORM_BETA = (8.0 * DEPTH) ** -0.25
EPS = 1e-5

kernel_name = "hymba_ssd_mla_peer_encoder"


def layer_norm(x, g, b):
    xf = x.astype(jnp.float32)
    mu = jnp.mean(xf, -1, keepdims=True)
    var = jnp.mean(jnp.square(xf - mu), -1, keepdims=True)
    return ((xf - mu) * lax.rsqrt(var + EPS) * g.astype(jnp.float32) + b.astype(jnp.float32)).astype(x.dtype)


def rms_norm(x, g):
    xf = x.astype(jnp.float32)
    ms = jnp.mean(jnp.square(xf), -1, keepdims=True)
    return (xf * lax.rsqrt(ms + EPS) * g.astype(jnp.float32)).astype(x.dtype)


def depthwise_conv(x, w, bias):
    c = x.shape[-1]
    y = lax.conv_general_dilated(x, w[:, None, :].astype(x.dtype), window_strides=(1,),
                                 padding=[(D_CONV // 2, D_CONV // 2)],
                                 dimension_numbers=("NWC", "WIO", "NWC"), feature_group_count=c)
    return y + bias.astype(x.dtype)


def segsum(a):
    t = a.shape[-1]
    cs = jnp.cumsum(a, -1)
    d = cs[..., :, None] - cs[..., None, :]
    mask = jnp.tril(jnp.ones((t, t), dtype=bool))
    return jnp.where(mask, d, -jnp.inf)


def ssd_chunked(xdt, da, bm, cm):
    b, lp = xdt.shape[:2]
    nc = lp // CHUNK
    x = xdt.reshape(b, nc, CHUNK, SSM_GROUPS, HEADS_PER_GROUP, SSM_HEAD_DIM)
    bc = bm.reshape(b, nc, CHUNK, SSM_GROUPS, D_STATE)
    cc = cm.reshape(b, nc, CHUNK, SSM_GROUPS, D_STATE)
    a = da.reshape(b, nc, CHUNK, SSM_GROUPS, HEADS_PER_GROUP).transpose(0, 3, 4, 1, 2)
    a_cs = jnp.cumsum(a, -1)
    lmat = jnp.exp(segsum(a))
    cb = jnp.einsum("bclgn,bcsgn->bcgls", cc, bc)
    y_diag = jnp.einsum("bcgls,bghcls,bcsghp->bclghp", cb, lmat, x)
    decay_states = jnp.exp(a_cs[..., -1:] - a_cs)
    states = jnp.einsum("bclgn,bghcl,bclghp->bcghpn", bc, decay_states, x)
    states = jnp.concatenate([jnp.zeros_like(states[:, :1]), states], axis=1)
    a_last = jnp.pad(a_cs[..., -1], ((0, 0), (0, 0), (0, 0), (1, 0)))
    chunk_decay = jnp.exp(segsum(a_last))
    new_states = jnp.einsum("bghzc,bcghpn->bzghpn", chunk_decay, states)
    prev_states = new_states[:, :-1]
    y_off = jnp.einsum("bclgn,bcghpn,bghcl->bclghp", cc, prev_states, jnp.exp(a_cs))
    return (y_diag + y_off).reshape(b, lp, SSM_GROUPS, HEADS_PER_GROUP, SSM_HEAD_DIM)


def rope_tables(pos):
    half = QK_ROPE_DIM // 2
    inv = ROPE_THETA ** (-jnp.arange(half, dtype=jnp.float32) / half)
    ang = pos[:, None] * inv[None, :]
    return jnp.cos(ang), jnp.sin(ang)


def apply_rope(x, cos, sin):
    x1, x2 = jnp.split(x, 2, axis=-1)
    return jnp.concatenate([x1 * cos - x2 * sin, x1 * sin + x2 * cos], -1).astype(x.dtype)


def block_attention(q_nope, q_rope, k_nope, k_rope, v):
    b, l, h, _ = q_nope.shape
    nb = -(-l // Q_BLOCK)
    pad = nb * Q_BLOCK - l

    def to_blocks(t):
        t = jnp.pad(t, ((0, 0), (0, pad), (0, 0), (0, 0)))
        return t.reshape(b, nb, Q_BLOCK, h, t.shape[-1]).transpose(1, 0, 2, 3, 4)

    scale = 1.0 / math.sqrt(QK_NOPE_DIM + QK_ROPE_DIM)

    def one_block(qb):
        qn, qr = qb
        s = jnp.einsum("bqhd,bkhd->bhqk", qn, k_nope) + jnp.einsum("bqhr,bkr->bhqk", qr, k_rope)
        pr = jax.nn.softmax(s.astype(jnp.float32) * scale, axis=-1)
        return jnp.einsum("bhqk,bkhd->bqhd", pr.astype(v.dtype), v)

    o = lax.map(one_block, (to_blocks(q_nope), to_blocks(q_rope)))
    return o.transpose(1, 0, 2, 3, 4).reshape(b, nb * Q_BLOCK, h, V_HEAD_DIM)[:, :l]


def pad_front(t):
    return jnp.pad(t, [(0, 0), (META_PAD, 0)] + [(0, 0)] * (t.ndim - 2))


def hybrid_mixer(h, pos, p):
    b, l, _ = h.shape
    proj = h @ p["w_in"]
    z = proj[..., :SPLIT_Z]
    xbc = proj[..., SPLIT_Z:SPLIT_XBC]
    dt_raw = proj[..., SPLIT_XBC:SPLIT_DT].astype(jnp.float32)
    c_q = proj[..., SPLIT_DT:SPLIT_CQ]
    c_kv = proj[..., SPLIT_CQ:SPLIT_CKV]
    k_rope = proj[..., SPLIT_CKV:]

    xbc = jax.nn.silu(depthwise_conv(xbc, p["conv_w"], p["conv_b"]))
    xs = xbc[..., :D_SSM].reshape(b, l, SSM_GROUPS, HEADS_PER_GROUP, SSM_HEAD_DIM)
    bm = xbc[..., D_SSM:D_SSM + SSM_GROUPS * D_STATE].reshape(b, l, SSM_GROUPS, D_STATE)
    cm = xbc[..., D_SSM + SSM_GROUPS * D_STATE:].reshape(b, l, SSM_GROUPS, D_STATE)
    gh = (SSM_GROUPS, HEADS_PER_GROUP)
    dt_f = jax.nn.softplus(dt_raw[..., :SSM_HEADS] + p["dt_bias_fwd"].astype(jnp.float32)).reshape(b, l, *gh)
    dt_b = jax.nn.softplus(dt_raw[..., SSM_HEADS:] + p["dt_bias_bwd"].astype(jnp.float32)).reshape(b, l, *gh)
    a_f = -jnp.exp(p["a_log_fwd"].astype(jnp.float32)).reshape(gh)
    a_b = -jnp.exp(p["a_log_bwd"].astype(jnp.float32)).reshape(gh)
    xs_p, bm_p, cm_p = pad_front(xs), pad_front(bm), pad_front(cm)
    dt_f_p, dt_b_p = pad_front(dt_f), pad_front(dt_b)
    y_f = ssd_chunked(xs_p * dt_f_p[..., None], dt_f_p * a_f, bm_p, cm_p)
    flip = lambda t: jnp.flip(t, axis=1)
    y_b = flip(ssd_chunked(flip(xs_p * dt_b_p[..., None]), flip(dt_b_p * a_b), flip(bm_p), flip(cm_p)))
    y = (y_f + y_b)[:, META_PAD:] + p["d_skip"].astype(jnp.float32).reshape(*gh, 1) * xs
    y = y.reshape(b, l, D_SSM).astype(h.dtype)
    y_ssm = rms_norm(y * jax.nn.silu(z), p["ssm_norm_g"])

    q = (rms_norm(c_q, p["q_norm_g"]) @ p["w_uq"]).reshape(b, l, MLA_HEADS, QK_NOPE_DIM + QK_ROPE_DIM)
    kv = (rms_norm(c_kv, p["kv_norm_g"]) @ p["w_ukv"]).reshape(b, l, MLA_HEADS, QK_NOPE_DIM + V_HEAD_DIM)
    cos, sin = rope_tables(pos)
    q_nope = q[..., :QK_NOPE_DIM]
    q_rope = apply_rope(q[..., QK_NOPE_DIM:], cos[:, None, :], sin[:, None, :])
    k_nope = kv[..., :QK_NOPE_DIM]
    v = kv[..., QK_NOPE_DIM:]
    k_rope = apply_rope(k_rope, cos, sin)
    o = block_attention(q_nope, q_rope, k_nope, k_rope, v).reshape(b, l, D_ATTN)
    y_attn = rms_norm(o.astype(h.dtype), p["attn_norm_g"])

    return jnp.concatenate([y_ssm, y_attn], axis=-1) @ p["w_out"]


def peer_ffn(h, p):
    b, l, d = h.shape
    t = b * l
    nblk = -(-t // PEER_BLOCK)
    xt = jnp.pad(h.reshape(t, d), ((0, nblk * PEER_BLOCK - t), (0, 0))).reshape(nblk, PEER_BLOCK, d)
    w_q, sub_keys, u_tab, v_tab = p["peer_w_query"], p["peer_sub_keys"], p["peer_u"], p["peer_v"]

    def block(xb):
        q = (xb @ w_q).reshape(PEER_BLOCK, PEER_HEADS, 2, D_SUBKEY)
        s1 = jnp.einsum("thd,hnd->thn", q[:, :, 0], sub_keys[0]).astype(jnp.float32)
        s2 = jnp.einsum("thd,hnd->thn", q[:, :, 1], sub_keys[1]).astype(jnp.float32)
        v1, i1 = lax.top_k(s1, PEER_TOPK)
        v2, i2 = lax.top_k(s2, PEER_TOPK)
        cand = (v1[..., :, None] + v2[..., None, :]).reshape(PEER_BLOCK, PEER_HEADS, PEER_TOPK * PEER_TOPK)
        cidx = (i1[..., :, None] * N_KEYS + i2[..., None, :]).reshape(PEER_BLOCK, PEER_HEADS, PEER_TOPK * PEER_TOPK)
        top_s, sel = lax.top_k(cand, PEER_TOPK)
        experts = jnp.take_along_axis(cidx, sel, axis=-1)
        g = jax.nn.softmax(top_s, axis=-1)
        u = jnp.take(u_tab, experts, axis=0)
        act = jax.nn.gelu(jnp.einsum("thkd,td->thk", u, xb).astype(jnp.float32), approximate=False)
        ve = jnp.take(v_tab, experts, axis=0)
        return jnp.einsum("thk,thkd->td", (g * act).astype(ve.dtype), ve)

    out = lax.map(block, xt).reshape(nblk * PEER_BLOCK, d)[:t]
    return out.reshape(b, l, d).astype(h.dtype)


def encode(x, meta_tokens, ln_in_g, ln_in_b, layer_params):
    b, s, _ = x.shape
    meta = jnp.broadcast_to(meta_tokens.astype(x.dtype)[None], (b, N_META, D_MODEL))
    h = layer_norm(jnp.concatenate([meta, x], axis=1), ln_in_g, ln_in_b)
    pos = jnp.arange(s + N_META, dtype=jnp.float32)
    for li in range(DEPTH):
        p = {k: w[li] for k, w in layer_params.items()}
        h = layer_norm(DEEPNORM_ALPHA * h + hybrid_mixer(h, pos, p), p["ln1_g"], p["ln1_b"])
        h = layer_norm(DEEPNORM_ALPHA * h + peer_ffn(h, p), p["ln2_g"], p["ln2_b"])
    return h[:, N_META:]


def setup_inputs(seed: int = 0) -> dict:
    key = jax.random.key(seed)
    ks = jax.random.split(key, 32)
    nrm = lambda k, shape, s: jax.random.normal(k, shape, jnp.float32) * s
    gain = lambda k, n: 1.0 + 0.02 * jax.random.normal(k, (DEPTH, n), jnp.float32)
    bias = lambda k, n: 0.02 * jax.random.normal(k, (DEPTH, n), jnp.float32)

    def dt_bias(k):
        dt = jnp.exp(jax.random.uniform(k, (DEPTH, SSM_HEADS), jnp.float32, math.log(1e-3), math.log(1e-1)))
        return dt + jnp.log(-jnp.expm1(-dt))

    return {
        "x_prompt": jax.random.normal(ks[0], (BATCH, SEQ, D_MODEL), jnp.float32),
        "x_sample": jax.random.normal(ks[1], (DEC_BATCH, DEC_SEQ, D_MODEL), jnp.float32),
        "meta_tokens": nrm(ks[2], (N_META, D_MODEL), 1.0),
        "ln_in_g": 1.0 + 0.02 * jax.random.normal(ks[3], (D_MODEL,), jnp.float32),
        "ln_in_b": 0.02 * jax.random.normal(ks[4], (D_MODEL,), jnp.float32),
        "w_in": nrm(ks[5], (DEPTH, D_MODEL, D_IN_PROJ), D_MODEL ** -0.5),
        "conv_w": nrm(ks[6], (DEPTH, D_CONV, D_CONV_CH), D_CONV ** -0.5),
        "conv_b": bias(ks[7], D_CONV_CH),
        "dt_bias_fwd": dt_bias(ks[8]),
        "dt_bias_bwd": dt_bias(ks[9]),
        "a_log_fwd": jnp.log(jax.random.uniform(ks[10], (DEPTH, SSM_HEADS), jnp.float32, 1.0, 16.0)),
        "a_log_bwd": jnp.log(jax.random.uniform(ks[11], (DEPTH, SSM_HEADS), jnp.float32, 1.0, 16.0)),
        "d_skip": gain(ks[12], SSM_HEADS),
        "ssm_norm_g": gain(ks[13], D_SSM),
        "q_norm_g": gain(ks[14], Q_LORA_RANK),
        "w_uq": nrm(ks[15], (DEPTH, Q_LORA_RANK, MLA_HEADS * (QK_NOPE_DIM + QK_ROPE_DIM)), Q_LORA_RANK ** -0.5),
        "kv_norm_g": gain(ks[16], KV_LORA_RANK),
        "w_ukv": nrm(ks[17], (DEPTH, KV_LORA_RANK, MLA_HEADS * (QK_NOPE_DIM + V_HEAD_DIM)), KV_LORA_RANK ** -0.5),
        "attn_norm_g": gain(ks[18], D_ATTN),
        "w_out": nrm(ks[19], (DEPTH, D_MIX, D_MODEL), DEEPNORM_BETA * D_MIX ** -0.5),
        "ln1_g": gain(ks[20], D_MODEL),
        "ln1_b": bias(ks[21], D_MODEL),
        "peer_w_query": nrm(ks[22], (DEPTH, D_MODEL, PEER_HEADS * D_KEY), D_MODEL ** -0.5),
        "peer_sub_keys": nrm(ks[23], (DEPTH, 2, PEER_HEADS, N_KEYS, D_SUBKEY), D_SUBKEY ** -0.5),
        "peer_u": nrm(ks[24], (DEPTH, N_EXPERTS, D_MODEL), D_MODEL ** -0.5),
        "peer_v": nrm(ks[25], (DEPTH, N_EXPERTS, D_MODEL), DEEPNORM_BETA * PEER_HEADS ** -0.5),
        "ln2_g": gain(ks[26], D_MODEL),
        "ln2_b": bias(ks[27], D_MODEL),
    }


def reference(x_prompt, x_sample, meta_tokens, ln_in_g, ln_in_b, w_in, conv_w, conv_b,
              dt_bias_fwd, dt_bias_bwd, a_log_fwd, a_log_bwd, d_skip, ssm_norm_g,
              q_norm_g, w_uq, kv_norm_g, w_ukv, attn_norm_g, w_out, ln1_g, ln1_b,
              peer_w_query, peer_sub_keys, peer_u, peer_v, ln2_g, ln2_b):
    layer_params = dict(w_in=w_in, conv_w=conv_w, conv_b=conv_b, dt_bias_fwd=dt_bias_fwd,
                        dt_bias_bwd=dt_bias_bwd, a_log_fwd=a_log_fwd, a_log_bwd=a_log_bwd,
                        d_skip=d_skip, ssm_norm_g=ssm_norm_g, q_norm_g=q_norm_g, w_uq=w_uq,
                        kv_norm_g=kv_norm_g, w_ukv=w_ukv, attn_norm_g=attn_norm_g, w_out=w_out,
                        ln1_g=ln1_g, ln1_b=ln1_b, peer_w_query=peer_w_query,
                        peer_sub_keys=peer_sub_keys, peer_u=peer_u, peer_v=peer_v,
                        ln2_g=ln2_g, ln2_b=ln2_b)
    y_prompt = encode(x_prompt, meta_tokens, ln_in_g, ln_in_b, layer_params)
    y_sample = encode(x_sample, meta_tokens, ln_in_g, ln_in_b, layer_params)
    return (y_prompt, y_sample)
```

```python
import functools
import math

import jax
import jax.numpy as jnp
from jax import lax
from jax.experimental import pallas as pl
from jax.experimental.pallas import tpu as pltpu

F32 = jnp.float32
BF16 = jnp.bfloat16
I32 = jnp.int32
U32 = jnp.uint32

D_MODEL = 1024
N_META = 16
SSM_HEADS = 16
SSM_HEAD_DIM = 64
D_SSM = SSM_HEADS * SSM_HEAD_DIM
SSM_GROUPS = 2
D_STATE = 128
D_CONV = 5
D_CONV_CH = D_SSM + 2 * SSM_GROUPS * D_STATE
CHUNK = 128
META_PAD = CHUNK - N_META
MLA_HEADS = 8
QK_NOPE_DIM = 64
QK_ROPE_DIM = 32
V_HEAD_DIM = 64
Q_LORA_RANK = 384
KV_LORA_RANK = 256
D_ATTN = MLA_HEADS * V_HEAD_DIM
ROPE_THETA = 10000.0
SPLIT_Z = D_SSM
SPLIT_XBC = SPLIT_Z + D_CONV_CH
SPLIT_DT = SPLIT_XBC + 2 * SSM_HEADS
SPLIT_CQ = SPLIT_DT + Q_LORA_RANK
SPLIT_CKV = SPLIT_CQ + KV_LORA_RANK
PEER_HEADS = 8
N_KEYS = 128
PEER_TOPK = 16
D_KEY = 256
D_SUBKEY = D_KEY // 2
DEPTH = 1
DEEPNORM_ALPHA = (2.0 * DEPTH) ** 0.25
EPS = 1e-5

LANES = 128
SUBLANES = 8
HEAD_PAD = LANES
D_HEADS_PAD = MLA_HEADS * HEAD_PAD
VMEM_LIMIT = 56 * 1024 * 1024
SSD_HEADS_PER_STEP = 4
PAIRS = PEER_HEADS * PEER_TOPK
SLAB = 4
HI_MASK = 0xFFFF0000

OFF_Z = 0
OFF_XBC = D_SSM
OFF_CQ = OFF_XBC + D_CONV_CH
OFF_CKV = OFF_CQ + Q_LORA_RANK
OFF_TAIL_A = OFF_CKV + KV_LORA_RANK
OFF_TAIL_B = OFF_TAIL_A + LANES
D_PROJ = OFF_TAIL_B + LANES
KR_LANE = 2 * SSM_HEADS


def _cparams(n_axes):
    return pltpu.CompilerParams(dimension_semantics=("arbitrary",) * n_axes,
                                vmem_limit_bytes=VMEM_LIMIT)


def _full(shape):
    zeros = (0,) * len(shape)
    return pl.BlockSpec(shape, lambda *_: zeros)


def _ln(x, g, b):
    mu = jnp.mean(x, axis=-1, keepdims=True)
    xc = x - mu
    var = jnp.mean(xc * xc, axis=-1, keepdims=True)
    return xc * lax.rsqrt(var + EPS) * g + b


def _rms(x, g, n):
    ms = jnp.sum(x * x, axis=-1, keepdims=True) * (1.0 / n)
    return x * lax.rsqrt(ms + EPS) * g


def _mm(a, b):
    return jnp.dot(a.astype(BF16), b.astype(BF16), preferred_element_type=F32)


def _mm_nt(a, b):
    return lax.dot_general(a.astype(BF16), b.astype(BF16), (((1,), (1,)), ((), ())),
                           preferred_element_type=F32)


def _mm_exact(a, b):
    return jnp.dot(a, b, preferred_element_type=F32, precision=lax.Precision.HIGHEST)


def _sigmoid(x):
    return 1.0 / (1.0 + jnp.exp(-x))


def _softplus(x):
    return jnp.maximum(x, 0.0) + jnp.log(1.0 + jnp.exp(-jnp.abs(x)))


def _inproj_kernel(x_ref, cq_ref, sq_ref, ck_ref, sk_ref, lng_ref, lnb_ref, win_ref, qng_ref,
                   wuq_ref, wuqr_ref, kvng_ref, wuk_ref, wuv_ref, ekr_ref,
                   z_ref, xbc_ref, tail_ref, q_ref, k_ref, v_ref, *, tm):
    j = pl.program_id(1)
    h = _ln(x_ref[0], lng_ref[...], lnb_ref[...])
    proj = _mm(h, win_ref[...])
    row = j * tm + lax.broadcasted_iota(I32, (tm, 1), 0)
    valid = row >= META_PAD
    z_ref[0] = proj[:, OFF_Z:OFF_XBC]
    xbc_ref[0] = jnp.where(valid, proj[:, OFF_XBC:OFF_CQ], 0.0)
    tail = proj[:, OFF_TAIL_A:OFF_TAIL_B]
    tail_ref[0] = tail
    cqn = _rms(proj[:, OFF_CQ:OFF_CKV], qng_ref[...], Q_LORA_RANK)
    qf = _mm(cqn, wuq_ref[...]) * cq_ref[...] + _mm(cqn, wuqr_ref[...]) * sq_ref[...]
    ckvn = _rms(proj[:, OFF_CKV:OFF_TAIL_A], kvng_ref[...], KV_LORA_RANK)
    kt = tail * ck_ref[...] + proj[:, OFF_TAIL_B:D_PROJ] * sk_ref[...]
    kf = _mm(ckvn, wuk_ref[...]) + _mm(kt, ekr_ref[...])
    vf = _mm(ckvn, wuv_ref[...])
    for hd in range(MLA_HEADS):
        sl = slice(hd * HEAD_PAD, (hd + 1) * HEAD_PAD)
        q_ref[0, hd] = qf[:, sl].astype(BF16)
        k_ref[0, hd] = kf[:, sl].astype(BF16)
        v_ref[0, hd] = vf[:, sl].astype(BF16)


def _inproj(xpad, tabs, w):
    b, lp, _ = xpad.shape
    tm = next(t for t in (544, 384, 256, 128) if lp % t == 0)
    grid = (b, lp // tm)
    row_spec = lambda width: pl.BlockSpec((tm, width), lambda i, j: (j, 0))
    seq_spec = lambda width: pl.BlockSpec((1, tm, width), lambda i, j: (i, j, 0))
    head_spec = pl.BlockSpec((1, MLA_HEADS, tm, HEAD_PAD), lambda i, j: (i, 0, j, 0))
    head_shape = jax.ShapeDtypeStruct((b, MLA_HEADS, lp, HEAD_PAD), BF16)
    return pl.pallas_call(
        functools.partial(_inproj_kernel, tm=tm),
        grid=grid,
        in_specs=[seq_spec(D_MODEL), row_spec(D_HEADS_PAD), row_spec(D_HEADS_PAD), row_spec(LANES),
                  row_spec(LANES), _full((1, D_MODEL)), _full((1, D_MODEL)), _full((D_MODEL, D_PROJ)),
                  _full((1, Q_LORA_RANK)), _full((Q_LORA_RANK, D_HEADS_PAD)),
                  _full((Q_LORA_RANK, D_HEADS_PAD)), _full((1, KV_LORA_RANK)),
                  _full((KV_LORA_RANK, D_HEADS_PAD)), _full((KV_LORA_RANK, D_HEADS_PAD)),
                  _full((LANES, D_HEADS_PAD))],
        out_specs=[seq_spec(D_SSM), seq_spec(D_CONV_CH), seq_spec(LANES), head_spec, head_spec, head_spec],
        out_shape=[jax.ShapeDtypeStruct((b, lp, D_SSM), F32),
                   jax.ShapeDtypeStruct((b, lp, D_CONV_CH), F32),
                   jax.ShapeDtypeStruct((b, lp, LANES), F32),
                   head_shape, head_shape, head_shape],
        compiler_params=_cparams(2),
        name="inproj",
    )(xpad, tabs["cos_q"], tabs["sin_q"], tabs["cos_k"], tabs["sin_k"], w["ln_in_g"], w["ln_in_b"],
      w["w_proj"], w["q_norm_g"], w["w_uq"], w["w_uq_rot"], w["kv_norm_g"], w["w_uk"], w["w_uv"],
      w["e_kr"])


def _expand_heads(v, lane0, n_heads):
    lane = lax.broadcasted_iota(I32, (v.shape[0], LANES), 1)
    tiles = []
    for t in range(n_heads // 2):
        a = v[:, lane0 + 2 * t:lane0 + 2 * t + 1]
        b = v[:, lane0 + 2 * t + 1:lane0 + 2 * t + 2]
        tiles.append(jnp.where(lane < SSM_HEAD_DIM, a, b))
    return jnp.concatenate(tiles, axis=1)


def _conv_silu_chunk(win, w, bias, o, n):
    acc = bias + w[0:1] * win[o - 2:o - 2 + n]
    for k in range(1, D_CONV):
        acc = acc + w[k:k + 1] * win[o + k - 2:o + k - 2 + n]
    return acc * _sigmoid(acc)


def _conv_silu(src_ref, w_ref, b_ref, dst, nc):
    w = w_ref[...]
    bias = b_ref[...]
    lp = nc * CHUNK
    halo = SUBLANES
    width = src_ref.shape[-1]
    win = src_ref[0, 0:CHUNK + 2 * halo, :]
    dst[0:halo, :] = jnp.zeros((halo, width), F32)
    dst[halo:CHUNK, :] = _conv_silu_chunk(win, w, bias, halo, CHUNK - halo)

    def mid(c, carry):
        start = pl.multiple_of(c * CHUNK - halo, halo)
        win = src_ref[0, pl.ds(start, CHUNK + 2 * halo), :]
        dst[pl.ds(pl.multiple_of(c * CHUNK, CHUNK), CHUNK), :] = _conv_silu_chunk(win, w, bias, halo, CHUNK)
        return carry

    lax.fori_loop(1, nc - 1, mid, 0)
    win = jnp.concatenate([src_ref[0, lp - CHUNK - 2 * halo:lp, :], jnp.zeros((halo, width), F32)], axis=0)
    dst[lp - CHUNK:lp, :] = _conv_silu_chunk(win, w, bias, 2 * halo, CHUNK)


def _ssd_kernel(xs_ref, bm_ref, cm_ref, z_ref, cwx_ref, cbx_ref, cwb_ref, cbb_ref, cwc_ref, cbc_ref,
                dtc_ref, dtr_ref, pcol_ref, prow_ref, out_ref,
                xs_s, b_s, c_s, y_s, st_s, dtc_s, dtr_s, *, hs, nc):
    lp = nc * CHUNK
    cw = hs * SSM_HEAD_DIM
    pc = pcol_ref[0]
    pr = prow_ref[0]

    _conv_silu(xs_ref, cwx_ref, cbx_ref, xs_s, nc)
    _conv_silu(bm_ref, cwb_ref, cbb_ref, b_s, nc)
    _conv_silu(cm_ref, cwc_ref, cbc_ref, c_s, nc)

    rowi = lax.broadcasted_iota(I32, (lp, 1), 0)
    dtc_s[...] = jnp.where(rowi >= META_PAD, _softplus(dtc_ref[0, 0] + pc[0:1]), 0.0)
    lanei = lax.broadcasted_iota(I32, (1, lp), 1)
    dtr_s[...] = jnp.where(lanei >= META_PAD, _softplus(dtr_ref[0, 0] + pr[:, 0:1]), 0.0)
    a_col = -jnp.exp(pc[1:2])
    a_row = -jnp.exp(pr[:, 1:2])

    ri = lax.broadcasted_iota(I32, (CHUNK, CHUNK), 0)
    ci = lax.broadcasted_iota(I32, (CHUNK, CHUNK), 1)

    def run(fwd):
        d0 = 0 if fwd else hs
        tri = (ri >= ci) if fwd else (ri <= ci)
        tmat = tri.astype(F32)
        umat = ((ci >= ri) if fwd else (ci <= ri)).astype(F32)
        st_s[...] = jnp.zeros((D_STATE, cw), F32)

        def body(i, carry):
            c = i if fwd else nc - 1 - i
            r0 = pl.multiple_of(c * CHUNK, CHUNK)
            dt_c = dtc_s[pl.ds(r0, CHUNK), :]
            cs_c = _mm_exact(tmat, dt_c * a_col)
            a_r = dtr_s[:, pl.ds(r0, CHUNK)] * a_row
            cs_r = _mm_exact(a_r, umat)
            tot = cs_c[CHUNK - 1:CHUNK] if fwd else cs_c[0:1]
            x = xs_s[pl.ds(r0, CHUNK), :]
            bc = b_s[pl.ds(r0, CHUNK), :]
            cc = c_s[pl.ds(r0, CHUNK), :]
            xdt = x * _expand_heads(dt_c, d0, hs)
            cb = _mm_nt(cc, bc)
            st = st_s[...]
            y = _mm(cc, st) * _expand_heads(jnp.exp(cs_c), d0, hs)
            lane = lax.broadcasted_iota(I32, (CHUNK, LANES), 1)
            tiles = []
            for t in range(hs // 2):
                xt = xdt[:, t * LANES:(t + 1) * LANES]
                ys = []
                for hh in (2 * t, 2 * t + 1):
                    seg = cs_c[:, d0 + hh:d0 + hh + 1] - cs_r[d0 + hh:d0 + hh + 1, :]
                    lmat = jnp.exp(jnp.where(tri, seg, -jnp.inf))
                    ys.append(_mm(cb * lmat, xt))
                tiles.append(jnp.where(lane < SSM_HEAD_DIM, ys[0], ys[1]))
            y = y + jnp.concatenate(tiles, axis=1)
            if fwd:
                y_s[pl.ds(r0, CHUNK), :] = y
            else:
                y_s[pl.ds(r0, CHUNK), :] = y_s[pl.ds(r0, CHUNK), :] + y
            xw = xdt * _expand_heads(jnp.exp(tot - cs_c), d0, hs)
            st_s[...] = st * _expand_heads(jnp.exp(tot), d0, hs) + _mm(bc.T, xw)
            return carry

        lax.fori_loop(0, nc, body, 0)

    run(True)
    run(False)

    d_skip = _expand_heads(pc[2:3], 0, hs)

    def fin(c, carry):
        r0 = pl.multiple_of(c * CHUNK, CHUNK)
        y = y_s[pl.ds(r0, CHUNK), :] + d_skip * xs_s[pl.ds(r0, CHUNK), :]
        zc = z_ref[0, pl.ds(r0, CHUNK), :]
        out_ref[0, pl.ds(pl.multiple_of(r0 - CHUNK, CHUNK), CHUNK), :] = y * (zc * _sigmoid(zc))
        return carry

    lax.fori_loop(1, nc, fin, 0)


def _ssd(z, xbc, dt_col, dt_row, w):
    b, lp, _ = z.shape
    nc = lp // CHUNK
    s = lp - CHUNK
    hs = SSD_HEADS_PER_STEP
    ng = SSM_HEADS // hs
    cw = hs * SSM_HEAD_DIM
    per_group = ng // SSM_GROUPS
    b_blk = D_SSM // D_STATE
    c_blk = b_blk + SSM_GROUPS
    x_map = lambda i, g: (i, 0, g)
    b_map = lambda i, g: (i, 0, b_blk + g // per_group)
    c_map = lambda i, g: (i, 0, c_blk + g // per_group)
    wx_map = lambda i, g: (0, g)
    wb_map = lambda i, g: (0, b_blk + g // per_group)
    wc_map = lambda i, g: (0, c_blk + g // per_group)
    return pl.pallas_call(
        functools.partial(_ssd_kernel, hs=hs, nc=nc),
        grid=(b, ng),
        in_specs=[pl.BlockSpec((1, lp, cw), x_map), pl.BlockSpec((1, lp, D_STATE), b_map),
                  pl.BlockSpec((1, lp, D_STATE), c_map), pl.BlockSpec((1, lp, cw), x_map),
                  pl.BlockSpec((D_CONV, cw), wx_map), pl.BlockSpec((1, cw), wx_map),
                  pl.BlockSpec((D_CONV, D_STATE), wb_map), pl.BlockSpec((1, D_STATE), wb_map),
                  pl.BlockSpec((D_CONV, D_STATE), wc_map), pl.BlockSpec((1, D_STATE), wc_map),
                  pl.BlockSpec((1, 1, lp, LANES), lambda i, g: (i, g, 0, 0)),
                  pl.BlockSpec((1, 1, 2 * hs, lp), lambda i, g: (i, g, 0, 0)),
                  pl.BlockSpec((1, 3, LANES), lambda i, g: (g, 0, 0)),
                  pl.BlockSpec((1, 2 * hs, 2), lambda i, g: (g, 0, 0))],
        out_specs=pl.BlockSpec((1, s, cw), x_map),
        out_shape=jax.ShapeDtypeStruct((b, s, D_SSM), F32),
        scratch_shapes=[pltpu.VMEM((lp, cw), F32), pltpu.VMEM((lp, D_STATE), F32),
                        pltpu.VMEM((lp, D_STATE), F32), pltpu.VMEM((lp, cw), F32),
                        pltpu.VMEM((D_STATE, cw), F32), pltpu.VMEM((lp, LANES), F32),
                        pltpu.VMEM((2 * hs, lp), F32)],
        compiler_params=_cparams(2),
        name="ssd",
    )(xbc, xbc, xbc, z, w["conv_w"], w["conv_b"], w["conv_w"], w["conv_b"], w["conv_w"], w["conv_b"],
      dt_col, dt_row, w["ssd_pcol"], w["ssd_prow"])


def _attn_kernel(q_ref, k_ref, v_ref, bias_ref, o_ref, *, tq):
    i = pl.program_id(2)
    r0 = pl.multiple_of(CHUNK + i * tq, CHUNK)
    q = q_ref[0, 0, pl.ds(r0, tq), :]
    s = lax.dot_general(q, k_ref[0, 0], (((1,), (1,)), ((), ())), preferred_element_type=F32)
    s = s + bias_ref[...]
    m = jnp.max(s, axis=-1, keepdims=True)
    p = jnp.exp(s - m)
    l = jnp.sum(p, axis=-1, keepdims=True)
    o = jnp.dot(p.astype(BF16), v_ref[0, 0], preferred_element_type=F32)
    o_ref[0] = o / l


def _attention(q, k, v, key_bias):
    b, nh, lp, _ = q.shape
    s = lp - CHUNK
    tq = next(t for t in (512, 256, 128) if s % t == 0)
    kv_spec = pl.BlockSpec((1, 1, lp, HEAD_PAD), lambda i, h, j: (i, h, 0, 0))
    return pl.pallas_call(
        functools.partial(_attn_kernel, tq=tq),
        grid=(b, nh, s // tq),
        in_specs=[kv_spec, kv_spec, kv_spec, _full((1, lp))],
        out_specs=pl.BlockSpec((1, tq, HEAD_PAD), lambda i, h, j: (i, j, h)),
        out_shape=jax.ShapeDtypeStruct((b, s, D_HEADS_PAD), F32),
        compiler_params=_cparams(3),
        name="attention",
    )(q, k, v, key_bias)


def _outproj_kernel(x_ref, y_ref, o_ref, lng_ref, lnb_ref, sg_ref, ag_ref, w1_ref, w2_ref,
                    g1_ref, b1_ref, h1_ref):
    h0 = _ln(x_ref[0], lng_ref[...], lnb_ref[...])
    ysn = _rms(y_ref[0], sg_ref[...], D_SSM)
    on = _rms(o_ref[0], ag_ref[...], D_ATTN)
    mix = _mm(ysn, w1_ref[...]) + _mm(on, w2_ref[...])
    h1_ref[0] = _ln(DEEPNORM_ALPHA * h0 + mix, g1_ref[...], b1_ref[...])


def _outproj(x, yg, o, w):
    b, s, _ = x.shape
    tb = next(t for t in (512, 256, 128) if s % t == 0)
    blk = lambda width: pl.BlockSpec((1, tb, width), lambda i, j: (i, j, 0))
    vec = lambda width: _full((1, width))
    return pl.pallas_call(
        _outproj_kernel,
        grid=(b, s // tb),
        in_specs=[blk(D_MODEL), blk(D_SSM), blk(D_HEADS_PAD), vec(D_MODEL), vec(D_MODEL), vec(D_SSM),
                  vec(D_HEADS_PAD), _full((D_SSM, D_MODEL)), _full((D_HEADS_PAD, D_MODEL)),
                  vec(D_MODEL), vec(D_MODEL)],
        out_specs=blk(D_MODEL),
        out_shape=jax.ShapeDtypeStruct((b, s, D_MODEL), F32),
        compiler_params=_cparams(2),
        name="outproj",
    )(x, yg, o, w["ln_in_g"], w["ln_in_b"], w["ssm_norm_g"], w["attn_norm_g"], w["w_out_ssm"],
      w["w_out_attn"], w["ln1_g"], w["ln1_b"])


def _topk_rows(s, vals_s, idx_s, row0, payload=None):
    n = s.shape[0]
    iota = lax.broadcasted_iota(I32, s.shape, 0)
    for r in range(PEER_TOPK):
        m = jnp.max(s, axis=0, keepdims=True)
        win = jnp.min(jnp.where(s == m, iota, n), axis=0, keepdims=True)
        hit = iota == win
        vals_s[row0 + r:row0 + r + 1, :] = m
        if payload is None:
            idx_s[row0 + r:row0 + r + 1, :] = win
        else:
            idx_s[row0 + r:row0 + r + 1, :] = jnp.sum(jnp.where(hit, payload, 0), axis=0, keepdims=True)
        s = jnp.where(hit, -jnp.inf, s)


_CAND_ROWS = [PEER_TOPK // (i + 1) for i in range(PEER_TOPK)]
_N_CAND = sum(_CAND_ROWS)
_N_CAND_PAD = -(-_N_CAND // SUBLANES) * SUBLANES


def _route_kernel(h_ref, wq_ref, keys_ref, idx_ref, g_ref,
                  v1_s, i1_s, v2_s, i2_s, cand_s, cidx_s, top_s, exp_s, gate_s, *, tb):
    q = _mm(h_ref[...], wq_ref[...]).astype(BF16)
    cand_s[...] = jnp.full((_N_CAND_PAD, tb), -jnp.inf, F32)
    cidx_s[...] = jnp.zeros((_N_CAND_PAD, tb), I32)
    for hd in range(PEER_HEADS):
        q1 = q[:, hd * D_KEY:hd * D_KEY + D_SUBKEY]
        q2 = q[:, hd * D_KEY + D_SUBKEY:(hd + 1) * D_KEY]
        s1 = lax.dot_general(keys_ref[0, hd], q1, (((1,), (1,)), ((), ())), preferred_element_type=F32)
        s2 = lax.dot_general(keys_ref[1, hd], q2, (((1,), (1,)), ((), ())), preferred_element_type=F32)
        _topk_rows(s1, v1_s, i1_s, 0)
        _topk_rows(s2, v2_s, i2_s, 0)
        off = 0
        for i, n in enumerate(_CAND_ROWS):
            cand_s[off:off + n, :] = v1_s[i:i + 1, :] + v2_s[0:n, :]
            cidx_s[off:off + n, :] = i1_s[i:i + 1, :] * N_KEYS + i2_s[0:n, :]
            off += n
        _topk_rows(cand_s[...], top_s, exp_s, hd * PEER_TOPK, payload=cidx_s[...])
        ts = top_s[hd * PEER_TOPK:(hd + 1) * PEER_TOPK, :]
        e = jnp.exp(ts - jnp.max(ts, axis=0, keepdims=True))
        gate_s[hd * PEER_TOPK:(hd + 1) * PEER_TOPK, :] = e / jnp.sum(e, axis=0, keepdims=True)
    idx_ref[...] = (exp_s[...] * SLAB).T
    g_ref[...] = gate_s[...].T


def _route(h1, w):
    t = h1.shape[0]
    tb = 256 if t % 256 == 0 else 128
    return pl.pallas_call(
        functools.partial(_route_kernel, tb=tb),
        grid=(t // tb,),
        in_specs=[pl.BlockSpec((tb, D_MODEL), lambda i: (i, 0)),
                  _full((D_MODEL, PEER_HEADS * D_KEY)),
                  _full((2, PEER_HEADS, N_KEYS, D_SUBKEY))],
        out_specs=[pl.BlockSpec((tb, PAIRS), lambda i: (i, 0)), pl.BlockSpec((tb, PAIRS), lambda i: (i, 0))],
        out_shape=[jax.ShapeDtypeStruct((t, PAIRS), I32), jax.ShapeDtypeStruct((t, PAIRS), F32)],
        scratch_shapes=[pltpu.VMEM((PEER_TOPK, tb), F32), pltpu.VMEM((PEER_TOPK, tb), I32),
                        pltpu.VMEM((PEER_TOPK, tb), F32), pltpu.VMEM((PEER_TOPK, tb), I32),
                        pltpu.VMEM((_N_CAND_PAD, tb), F32), pltpu.VMEM((_N_CAND_PAD, tb), I32),
                        pltpu.VMEM((PAIRS, tb), F32), pltpu.VMEM((PAIRS, tb), I32),
                        pltpu.VMEM((PAIRS, tb), F32)],
        compiler_params=_cparams(1),
        name="peer_route",
    )(h1, w["peer_w_query"], w["peer_sub_keys"])


def _unpack(slab):
    lo = lax.bitcast_convert_type(slab << 16, F32)
    hi = lax.bitcast_convert_type(slab & jnp.uint32(HI_MASK), F32)
    return lo, hi


def _gelu(x):
    return 0.5 * x * (1.0 + lax.erf(x * (1.0 / math.sqrt(2.0))))


def _peer_u_kernel(idx_ref, x_ref, g_ref, tab_ref, w_ref, prod_s, *, tb):
    ones = jnp.ones((LANES, LANES), BF16)
    ones_rows = jnp.ones((SUBLANES, PAIRS * SLAB), BF16)
    ri = lax.broadcasted_iota(I32, (PAIRS * SLAB, LANES), 0)
    ci = lax.broadcasted_iota(I32, (PAIRS * SLAB, LANES), 1)
    own = ri // SLAB == ci
    sub = lax.broadcasted_iota(I32, (SUBLANES, LANES), 0)

    def group(gi, carry):
        def token(ti, acts):
            t = gi * SUBLANES + ti
            xt = x_ref[pl.ds(pl.multiple_of(t * SUBLANES, SUBLANES), SUBLANES), :]
            x_lo = xt[0:SLAB]
            x_hi = xt[SLAB:2 * SLAB]
            base = t * PAIRS
            for k in range(PAIRS):
                row = pl.multiple_of(idx_ref[base + k], SLAB)
                lo, hi = _unpack(tab_ref[pl.ds(row, SLAB), :])
                prod_s[k * SLAB:(k + 1) * SLAB, :] = lo * x_lo + hi * x_hi
            lane_sums = jnp.dot(prod_s[...].astype(BF16), ones, preferred_element_type=F32)
            act = jnp.dot(ones_rows, jnp.where(own, lane_sums, 0.0).astype(BF16),
                          preferred_element_type=F32)
            return jnp.where(sub == ti, act, acts)

        acts = lax.fori_loop(0, SUBLANES, token, jnp.zeros((SUBLANES, LANES), F32))
        rows = pl.ds(pl.multiple_of(gi * SUBLANES, SUBLANES), SUBLANES)
        w_ref[rows, :] = g_ref[rows, :] * _gelu(acts)
        return carry

    lax.fori_loop(0, tb // SUBLANES, group, 0)


def _peer_v_kernel(idx_ref, wt_ref, tab_ref, out_ref, *, tb):
    n_acc = 4

    def token(t, carry):
        base = t * PAIRS
        acc_lo = [jnp.zeros((SLAB, LANES), F32) for _ in range(n_acc)]
        acc_hi = [jnp.zeros((SLAB, LANES), F32) for _ in range(n_acc)]
        for k in range(PAIRS):
            row = pl.multiple_of(idx_ref[base + k], SLAB)
            wk = wt_ref[base + k]
            lo, hi = _unpack(tab_ref[pl.ds(row, SLAB), :])
            acc_lo[k % n_acc] = acc_lo[k % n_acc] + wk * lo
            acc_hi[k % n_acc] = acc_hi[k % n_acc] + wk * hi
        lo = (acc_lo[0] + acc_lo[1]) + (acc_lo[2] + acc_lo[3])
        hi = (acc_hi[0] + acc_hi[1]) + (acc_hi[2] + acc_hi[3])
        out_ref[pl.ds(pl.multiple_of(t * SUBLANES, SUBLANES), SUBLANES), :] = jnp.concatenate([lo, hi], axis=0)
        return carry

    lax.fori_loop(0, tb, token, 0)


def _peer_tb(t):
    return next(c for c in (128, 64, 32, 16, 8) if t % c == 0)


def _table_spec(n_rows):
    return pl.BlockSpec((n_rows, LANES), lambda i: (0, 0), pipeline_mode=pl.Buffered(1))


def _peer_u(idx_flat, x8, gates, table):
    t = gates.shape[0]
    tb = _peer_tb(t)
    return pl.pallas_call(
        functools.partial(_peer_u_kernel, tb=tb),
        grid=(t // tb,),
        in_specs=[pl.BlockSpec((tb * PAIRS,), lambda i: (i,), memory_space=pltpu.SMEM),
                  pl.BlockSpec((tb * SUBLANES, LANES), lambda i: (i, 0)),
                  pl.BlockSpec((tb, PAIRS), lambda i: (i, 0)),
                  _table_spec(table.shape[0])],
        out_specs=pl.BlockSpec((tb, PAIRS), lambda i: (i, 0)),
        out_shape=jax.ShapeDtypeStruct((t, PAIRS), F32),
        scratch_shapes=[pltpu.VMEM((PAIRS * SLAB, LANES), F32)],
        compiler_params=_cparams(1),
        name="peer_u",
    )(idx_flat, x8, gates, table)


def _peer_v(idx_flat, wt_flat, table):
    t = idx_flat.shape[0] // PAIRS
    tb = _peer_tb(t)
    return pl.pallas_call(
        functools.partial(_peer_v_kernel, tb=tb),
        grid=(t // tb,),
        in_specs=[pl.BlockSpec((tb * PAIRS,), lambda i: (i,), memory_space=pltpu.SMEM),
                  pl.BlockSpec((tb * PAIRS,), lambda i: (i,), memory_space=pltpu.SMEM),
                  _table_spec(table.shape[0])],
        out_specs=pl.BlockSpec((tb * SUBLANES, LANES), lambda i: (i, 0)),
        out_shape=jax.ShapeDtypeStruct((t * SUBLANES, LANES), F32),
        compiler_params=_cparams(1),
        name="peer_v",
    )(idx_flat, wt_flat, table)


def _ln2_kernel(h_ref, p_ref, g_ref, b_ref, o_ref):
    o_ref[...] = _ln(DEEPNORM_ALPHA * h_ref[...] + p_ref[...], g_ref[...], b_ref[...])


def _ln2(h1, peer, w):
    t = h1.shape[0]
    tb = next(c for c in (512, 256, 128) if t % c == 0)
    blk = pl.BlockSpec((tb, D_MODEL), lambda i: (i, 0))
    return pl.pallas_call(
        _ln2_kernel,
        grid=(t // tb,),
        in_specs=[blk, blk, _full((1, D_MODEL)), _full((1, D_MODEL))],
        out_specs=blk,
        out_shape=jax.ShapeDtypeStruct((t, D_MODEL), F32),
        compiler_params=_cparams(1),
        name="ln2",
    )(h1, peer, w["ln2_g"], w["ln2_b"])


def _pack_table(tab):
    bits = lax.bitcast_convert_type(tab.astype(BF16), jnp.uint16).astype(U32)
    half = D_MODEL // 2
    words = bits[:, :half] | (bits[:, half:] << 16)
    return words.reshape(tab.shape[0] * SLAB, LANES)


def _rot_cols(wr):
    half = QK_ROPE_DIM // 2
    return jnp.concatenate([-wr[..., half:], wr[..., :half]], axis=-1)


def _prep_weights(ln_in_g, ln_in_b, w_in, conv_w, conv_b, dt_bias_fwd, dt_bias_bwd, a_log_fwd, a_log_bwd,
                  d_skip, ssm_norm_g, q_norm_g, w_uq, kv_norm_g, w_ukv, attn_norm_g, w_out, ln1_g, ln1_b,
                  peer_w_query, peer_sub_keys, peer_u, peer_v, ln2_g, ln2_b):
    li = 0
    row = lambda v: v.reshape(1, -1).astype(F32)
    w = {}
    w["ln_in_g"], w["ln_in_b"] = row(ln_in_g), row(ln_in_b)
    wi = w_in[li]
    w_kr = wi[:, SPLIT_CKV:]
    zc = lambda n: jnp.zeros((D_MODEL, n), F32)
    tail_a = jnp.concatenate([wi[:, SPLIT_XBC:SPLIT_DT], w_kr, zc(LANES - KR_LANE - QK_ROPE_DIM)], axis=1)
    tail_b = jnp.concatenate([zc(KR_LANE), _rot_cols(w_kr), zc(LANES - KR_LANE - QK_ROPE_DIM)], axis=1)
    w["w_proj"] = jnp.concatenate([wi[:, :SPLIT_XBC], wi[:, SPLIT_DT:SPLIT_CKV], tail_a, tail_b],
                                  axis=1).astype(BF16)
    w["conv_w"], w["conv_b"] = conv_w[li].astype(F32), row(conv_b[li])
    hs = SSD_HEADS_PER_STEP
    ng = SSM_HEADS // hs
    per_dir = lambda f, b_: jnp.concatenate([f[li].reshape(ng, hs), b_[li].reshape(ng, hs)], axis=1)
    dtb, alog = per_dir(dt_bias_fwd, dt_bias_bwd), per_dir(a_log_fwd, a_log_bwd)
    dsk = per_dir(d_skip, d_skip)
    pcol = jnp.stack([dtb, alog, dsk], axis=1)
    w["ssd_pcol"] = jnp.pad(pcol, ((0, 0), (0, 0), (0, LANES - 2 * hs))).astype(F32)
    w["ssd_prow"] = jnp.stack([dtb, alog], axis=2).astype(F32)
    w["ssm_norm_g"] = row(ssm_norm_g[li])
    w["q_norm_g"], w["kv_norm_g"] = row(q_norm_g[li]), row(kv_norm_g[li])
    pad_h = lambda a, n: jnp.pad(a, ((0, 0), (0, 0), (0, HEAD_PAD - n)))
    uq = w_uq[li].reshape(Q_LORA_RANK, MLA_HEADS, QK_NOPE_DIM + QK_ROPE_DIM)
    uq_rot = jnp.concatenate([jnp.zeros_like(uq[..., :QK_NOPE_DIM]), _rot_cols(uq[..., QK_NOPE_DIM:])], axis=-1)
    w["w_uq"] = pad_h(uq, QK_NOPE_DIM + QK_ROPE_DIM).reshape(Q_LORA_RANK, D_HEADS_PAD).astype(BF16)
    w["w_uq_rot"] = pad_h(uq_rot, QK_NOPE_DIM + QK_ROPE_DIM).reshape(Q_LORA_RANK, D_HEADS_PAD).astype(BF16)
    ukv = w_ukv[li].reshape(KV_LORA_RANK, MLA_HEADS, QK_NOPE_DIM + V_HEAD_DIM)
    w["w_uk"] = pad_h(ukv[..., :QK_NOPE_DIM], QK_NOPE_DIM).reshape(KV_LORA_RANK, D_HEADS_PAD).astype(BF16)
    w["w_uv"] = pad_h(ukv[..., QK_NOPE_DIM:], V_HEAD_DIM).reshape(KV_LORA_RANK, D_HEADS_PAD).astype(BF16)
    src = KR_LANE + jnp.arange(QK_ROPE_DIM)
    e_kr = jnp.zeros((LANES, MLA_HEADS, HEAD_PAD), F32)
    e_kr = e_kr.at[src, :, QK_NOPE_DIM + jnp.arange(QK_ROPE_DIM)].set(1.0)
    w["e_kr"] = e_kr.reshape(LANES, D_HEADS_PAD).astype(BF16)
    w["attn_norm_g"] = pad_h(attn_norm_g[li].reshape(1, MLA_HEADS, V_HEAD_DIM), V_HEAD_DIM).reshape(1, D_HEADS_PAD)
    wo = w_out[li]
    w["w_out_ssm"] = wo[:D_SSM].astype(BF16)
    wo_attn = wo[D_SSM:].reshape(MLA_HEADS, V_HEAD_DIM, D_MODEL)
    w["w_out_attn"] = jnp.pad(wo_attn, ((0, 0), (0, HEAD_PAD - V_HEAD_DIM), (0, 0))).reshape(
        D_HEADS_PAD, D_MODEL).astype(BF16)
    w["ln1_g"], w["ln1_b"] = row(ln1_g[li]), row(ln1_b[li])
    w["peer_w_query"] = peer_w_query[li].astype(BF16)
    w["peer_sub_keys"] = peer_sub_keys[li].astype(BF16)
    w["u_table"] = _pack_table(peer_u[li])
    w["v_table"] = _pack_table(peer_v[li])
    w["ln2_g"], w["ln2_b"] = row(ln2_g[li]), row(ln2_b[li])
    return w


def _rope_tables(lp):
    half = QK_ROPE_DIM // 2
    pos = jnp.arange(lp, dtype=F32) - float(META_PAD)
    inv = ROPE_THETA ** (-jnp.arange(half, dtype=F32) / half)
    ang = pos[:, None] * inv[None, :]
    cos2 = jnp.tile(jnp.cos(ang), (1, 2))
    sin2 = jnp.tile(jnp.sin(ang), (1, 2))
    scale = 1.0 / math.sqrt(QK_NOPE_DIM + QK_ROPE_DIM)
    tail = HEAD_PAD - QK_NOPE_DIM - QK_ROPE_DIM
    cq = jnp.concatenate([jnp.ones((lp, QK_NOPE_DIM), F32), cos2, jnp.zeros((lp, tail), F32)], axis=1)
    sq = jnp.concatenate([jnp.zeros((lp, QK_NOPE_DIM), F32), sin2, jnp.zeros((lp, tail), F32)], axis=1)
    place = lambda a: jnp.pad(a, ((0, 0), (KR_LANE, LANES - KR_LANE - QK_ROPE_DIM)))
    key_bias = jnp.where(jnp.arange(lp) >= META_PAD, 0.0, -jnp.inf).astype(F32).reshape(1, lp)
    return {"cos_q": jnp.tile(cq, (1, MLA_HEADS)) * scale, "sin_q": jnp.tile(sq, (1, MLA_HEADS)) * scale,
            "cos_k": place(cos2), "sin_k": place(sin2), "key_bias": key_bias}


def _encode(x, meta_tokens, w):
    b, s, _ = x.shape
    lp = s + CHUNK
    lead = jnp.concatenate([jnp.zeros((META_PAD, D_MODEL), F32), meta_tokens.astype(F32)], axis=0)
    xpad = jnp.concatenate([jnp.broadcast_to(lead[None], (b, CHUNK, D_MODEL)), x], axis=1)
    tabs = _rope_tables(lp)
    z, xbc, tail, q, k, v = _inproj(xpad, tabs, w)
    hs = SSD_HEADS_PER_STEP
    ng = SSM_HEADS // hs
    dt = tail[..., :2 * SSM_HEADS].reshape(b, lp, 2, ng, hs)
    dt = jnp.transpose(dt, (0, 3, 1, 2, 4)).reshape(b, ng, lp, 2 * hs)
    dt_col = jnp.pad(dt, ((0, 0), (0, 0), (0, 0), (0, LANES - 2 * hs)))
    dt_row = jnp.swapaxes(dt, 2, 3)
    yg = _ssd(z, xbc, dt_col, dt_row, w)
    o = _attention(q, k, v, tabs["key_bias"])
    h1 = _outproj(x, yg, o, w).reshape(b * s, D_MODEL)
    idx, gates = _route(h1, w)
    idx_flat = idx.reshape(-1)
    wt = _peer_u(idx_flat, h1.reshape(b * s * SUBLANES, LANES), gates, w["u_table"])
    peer = _peer_v(idx_flat, wt.reshape(-1), w["v_table"])
    return _ln2(h1, peer.reshape(b * s, D_MODEL), w).reshape(b, s, D_MODEL)


def kernel(x_prompt, x_sample, meta_tokens, ln_in_g, ln_in_b, w_in, conv_w, conv_b, dt_bias_fwd, dt_bias_bwd,
           a_log_fwd, a_log_bwd, d_skip, ssm_norm_g, q_norm_g, w_uq, kv_norm_g, w_ukv, attn_norm_g, w_out,
           ln1_g, ln1_b, peer_w_query, peer_sub_keys, peer_u, peer_v, ln2_g, ln2_b):
    w = _prep_weights(ln_in_g, ln_in_b, w_in, conv_w, conv_b, dt_bias_fwd, dt_bias_bwd, a_log_fwd, a_log_bwd,
                      d_skip, ssm_norm_g, q_norm_g, w_uq, kv_norm_g, w_ukv, attn_norm_g, w_out, ln1_g, ln1_b,
                      peer_w_query, peer_sub_keys, peer_u, peer_v, ln2_g, ln2_b)
    return (_encode(x_prompt, meta_tokens, w), _encode(x_sample, meta_tokens, w))
```

```python
import functools
import math

import jax
import jax.numpy as jnp
from jax import lax
from jax.experimental import pallas as pl
from jax.experimental.pallas import tpu as pltpu

F32 = jnp.float32
BF16 = jnp.bfloat16
I32 = jnp.int32
U32 = jnp.uint32

D_MODEL = 1024
N_META = 16
SSM_HEADS = 16
SSM_HEAD_DIM = 64
D_SSM = SSM_HEADS * SSM_HEAD_DIM
SSM_GROUPS = 2
D_STATE = 128
D_CONV = 5
D_CONV_CH = D_SSM + 2 * SSM_GROUPS * D_STATE
CHUNK = 128
META_PAD = CHUNK - N_META
MLA_HEADS = 8
QK_NOPE_DIM = 64
QK_ROPE_DIM = 32
V_HEAD_DIM = 64
Q_LORA_RANK = 384
KV_LORA_RANK = 256
D_ATTN = MLA_HEADS * V_HEAD_DIM
ROPE_THETA = 10000.0
SPLIT_Z = D_SSM
SPLIT_XBC = SPLIT_Z + D_CONV_CH
SPLIT_DT = SPLIT_XBC + 2 * SSM_HEADS
SPLIT_CQ = SPLIT_DT + Q_LORA_RANK
SPLIT_CKV = SPLIT_CQ + KV_LORA_RANK
PEER_HEADS = 8
N_KEYS = 128
PEER_TOPK = 16
D_KEY = 256
D_SUBKEY = D_KEY // 2
DEPTH = 1
DEEPNORM_ALPHA = (2.0 * DEPTH) ** 0.25
EPS = 1e-5

LANES = 128
SUBLANES = 8
HEAD_PAD = LANES
D_HEADS_PAD = MLA_HEADS * HEAD_PAD
VMEM_LIMIT = 56 * 1024 * 1024
SSD_HEADS_PER_STEP = 8
SSD_UNROLL = 2
ATTN_SPLIT = 2
PAIRS = PEER_HEADS * PEER_TOPK
SLAB = 4
PAIR_ROWS = PAIRS * SLAB
U_PHASE = SUBLANES

OFF_Z = 0
OFF_XBC = D_SSM
OFF_CQ = OFF_XBC + D_CONV_CH
OFF_CKV = OFF_CQ + Q_LORA_RANK
OFF_TAIL_A = OFF_CKV + KV_LORA_RANK
OFF_TAIL_B = OFF_TAIL_A + LANES
D_PROJ = OFF_TAIL_B + LANES
KR_LANE = 2 * SSM_HEADS


def _cparams(n_axes):
    return pltpu.CompilerParams(dimension_semantics=("arbitrary",) * n_axes,
                                vmem_limit_bytes=VMEM_LIMIT)


def _full(shape):
    zeros = (0,) * len(shape)
    return pl.BlockSpec(shape, lambda *_: zeros)


def _ln(x, g, b):
    mu = jnp.mean(x, axis=-1, keepdims=True)
    xc = x - mu
    var = jnp.mean(xc * xc, axis=-1, keepdims=True)
    return xc * lax.rsqrt(var + EPS) * g + b


def _rms(x, g, n):
    ms = jnp.sum(x * x, axis=-1, keepdims=True) * (1.0 / n)
    return x * lax.rsqrt(ms + EPS) * g


def _mm(a, b):
    return jnp.dot(a.astype(BF16), b.astype(BF16), preferred_element_type=F32)


def _mm_nt(a, b):
    return lax.dot_general(a.astype(BF16), b.astype(BF16), (((1,), (1,)), ((), ())),
                           preferred_element_type=F32)


def _mm_exact(a, b):
    return jnp.dot(a, b, preferred_element_type=F32, precision=lax.Precision.HIGHEST)


def _sigmoid(x):
    return 1.0 / (1.0 + jnp.exp(-x))


def _softplus(x):
    return jnp.maximum(x, 0.0) + jnp.log(1.0 + jnp.exp(-jnp.abs(x)))


def _inproj_kernel(x_ref, cq_ref, sq_ref, ck_ref, sk_ref, lng_ref, lnb_ref, win_ref, qng_ref,
                   wuq_ref, wuqr_ref, kvng_ref, wuk_ref, wuv_ref, ekr_ref,
                   z_ref, xbc_ref, tail_ref, q_ref, k_ref, v_ref, *, tm):
    j = pl.program_id(1)
    h = _ln(x_ref[0], lng_ref[...], lnb_ref[...])
    proj = _mm(h, win_ref[...])
    row = j * tm + lax.broadcasted_iota(I32, (tm, 1), 0)
    valid = row >= META_PAD
    z_ref[0] = proj[:, OFF_Z:OFF_XBC]
    xbc_ref[0] = jnp.where(valid, proj[:, OFF_XBC:OFF_CQ], 0.0)
    tail = proj[:, OFF_TAIL_A:OFF_TAIL_B]
    tail_ref[0] = tail
    cqn = _rms(proj[:, OFF_CQ:OFF_CKV], qng_ref[...], Q_LORA_RANK)
    qf = _mm(cqn, wuq_ref[...]) * cq_ref[...] + _mm(cqn, wuqr_ref[...]) * sq_ref[...]
    ckvn = _rms(proj[:, OFF_CKV:OFF_TAIL_A], kvng_ref[...], KV_LORA_RANK)
    kt = tail * ck_ref[...] + proj[:, OFF_TAIL_B:D_PROJ] * sk_ref[...]
    kf = _mm(ckvn, wuk_ref[...]) + _mm(kt, ekr_ref[...])
    vf = _mm(ckvn, wuv_ref[...])
    for hd in range(MLA_HEADS):
        sl = slice(hd * HEAD_PAD, (hd + 1) * HEAD_PAD)
        q_ref[0, hd] = qf[:, sl].astype(BF16)
        k_ref[0, hd] = kf[:, sl].astype(BF16)
        v_ref[0, hd] = vf[:, sl].astype(BF16)


def _inproj(xpad, tabs, w):
    b, lp, _ = xpad.shape
    tm = next(t for t in (544, 384, 256, 128) if lp % t == 0)
    grid = (b, lp // tm)
    row_spec = lambda width: pl.BlockSpec((tm, width), lambda i, j: (j, 0))
    seq_spec = lambda width: pl.BlockSpec((1, tm, width), lambda i, j: (i, j, 0))
    head_spec = pl.BlockSpec((1, MLA_HEADS, tm, HEAD_PAD), lambda i, j: (i, 0, j, 0))
    head_shape = jax.ShapeDtypeStruct((b, MLA_HEADS, lp, HEAD_PAD), BF16)
    return pl.pallas_call(
        functools.partial(_inproj_kernel, tm=tm),
        grid=grid,
        in_specs=[seq_spec(D_MODEL), row_spec(D_HEADS_PAD), row_spec(D_HEADS_PAD), row_spec(LANES),
                  row_spec(LANES), _full((1, D_MODEL)), _full((1, D_MODEL)), _full((D_MODEL, D_PROJ)),
                  _full((1, Q_LORA_RANK)), _full((Q_LORA_RANK, D_HEADS_PAD)),
                  _full((Q_LORA_RANK, D_HEADS_PAD)), _full((1, KV_LORA_RANK)),
                  _full((KV_LORA_RANK, D_HEADS_PAD)), _full((KV_LORA_RANK, D_HEADS_PAD)),
                  _full((LANES, D_HEADS_PAD))],
        out_specs=[seq_spec(D_SSM), seq_spec(D_CONV_CH), seq_spec(LANES), head_spec, head_spec, head_spec],
        out_shape=[jax.ShapeDtypeStruct((b, lp, D_SSM), F32),
                   jax.ShapeDtypeStruct((b, lp, D_CONV_CH), F32),
                   jax.ShapeDtypeStruct((b, lp, LANES), F32),
                   head_shape, head_shape, head_shape],
        compiler_params=_cparams(2),
        name="inproj",
    )(xpad, tabs["cos_q"], tabs["sin_q"], tabs["cos_k"], tabs["sin_k"], w["ln_in_g"], w["ln_in_b"],
      w["w_proj"], w["q_norm_g"], w["w_uq"], w["w_uq_rot"], w["kv_norm_g"], w["w_uk"], w["w_uv"],
      w["e_kr"])


def _expand_heads(v, lane0, n_heads):
    lane = lax.broadcasted_iota(I32, (v.shape[0], LANES), 1)
    tiles = []
    for t in range(n_heads // 2):
        a = v[:, lane0 + 2 * t:lane0 + 2 * t + 1]
        b = v[:, lane0 + 2 * t + 1:lane0 + 2 * t + 2]
        tiles.append(jnp.where(lane < SSM_HEAD_DIM, a, b))
    return jnp.concatenate(tiles, axis=1)


def _conv_silu_chunk(win, w, bias, o, n):
    acc = bias + w[0:1] * win[o - 2:o - 2 + n]
    for k in range(1, D_CONV):
        acc = acc + w[k:k + 1] * win[o + k - 2:o + k - 2 + n]
    return acc * _sigmoid(acc)


def _conv_silu(src_ref, w_ref, b_ref, dst, nc):
    w = w_ref[...]
    bias = b_ref[...]
    lp = nc * CHUNK
    halo = SUBLANES
    width = src_ref.shape[-1]
    win = src_ref[0, 0:CHUNK + 2 * halo, :]
    dst[0:halo, :] = jnp.zeros((halo, width), F32)
    dst[halo:CHUNK, :] = _conv_silu_chunk(win, w, bias, halo, CHUNK - halo)

    def mid(c, carry):
        start = pl.multiple_of(c * CHUNK - halo, halo)
        win = src_ref[0, pl.ds(start, CHUNK + 2 * halo), :]
        dst[pl.ds(pl.multiple_of(c * CHUNK, CHUNK), CHUNK), :] = _conv_silu_chunk(win, w, bias, halo, CHUNK)
        return carry

    lax.fori_loop(1, nc - 1, mid, 0)
    win = jnp.concatenate([src_ref[0, lp - CHUNK - 2 * halo:lp, :], jnp.zeros((halo, width), F32)], axis=0)
    dst[lp - CHUNK:lp, :] = _conv_silu_chunk(win, w, bias, 2 * halo, CHUNK)


def _ssd_kernel(xs_ref, bm_ref, cm_ref, z_ref, cwx_ref, cbx_ref, cwb_ref, cbb_ref, cwc_ref, cbc_ref,
                dtc_ref, dtr_ref, pcol_ref, prow_ref, out_ref,
                xs_s, b_s, c_s, yf_s, yb_s, stf_s, stb_s, dtc_s, dtr_s, *, hs, nc):
    lp = nc * CHUNK
    cw = hs * SSM_HEAD_DIM
    pc = pcol_ref[0]
    pr = prow_ref[0]

    _conv_silu(xs_ref, cwx_ref, cbx_ref, xs_s, nc)
    _conv_silu(bm_ref, cwb_ref, cbb_ref, b_s, nc)
    _conv_silu(cm_ref, cwc_ref, cbc_ref, c_s, nc)

    rowi = lax.broadcasted_iota(I32, (lp, 1), 0)
    dtc_s[...] = jnp.where(rowi >= META_PAD, _softplus(dtc_ref[0, 0] + pc[0:1]), 0.0)
    lanei = lax.broadcasted_iota(I32, (1, lp), 1)
    dtr_s[...] = jnp.where(lanei >= META_PAD, _softplus(dtr_ref[0, 0] + pr[:, 0:1]), 0.0)
    a_col = -jnp.exp(pc[1:2])
    a_exp_f = _expand_heads(a_col, 0, hs)
    a_exp_b = _expand_heads(a_col, hs, hs)
    a_row = -jnp.exp(pr[:, 1:2])

    ri = lax.broadcasted_iota(I32, (CHUNK, CHUNK), 0)
    ci = lax.broadcasted_iota(I32, (CHUNK, CHUNK), 1)

    lane = lax.broadcasted_iota(I32, (CHUNK, LANES), 1)
    lower = ri >= ci
    upper = ri <= ci
    lower_f = lower.astype(F32)
    upper_f = upper.astype(F32)
    stf_s[...] = jnp.zeros((D_STATE, cw), F32)
    stb_s[...] = jnp.zeros((D_STATE, cw), F32)

    def scan_chunk(c, fwd, st_ref, y_ref):
        d0 = 0 if fwd else hs
        tri = lower if fwd else upper
        tmat = lower_f if fwd else upper_f
        umat = upper_f if fwd else lower_f
        r0 = pl.multiple_of(c * CHUNK, CHUNK)
        dte = _expand_heads(dtc_s[pl.ds(r0, CHUNK), :], d0, hs)
        cs_e = _mm_exact(tmat, dte * (a_exp_f if fwd else a_exp_b))
        a_r = dtr_s[:, pl.ds(r0, CHUNK)] * a_row
        cs_r = _mm_exact(a_r, umat)
        tot = cs_e[CHUNK - 1:CHUNK] if fwd else cs_e[0:1]
        x = xs_s[pl.ds(r0, CHUNK), :]
        bc = b_s[pl.ds(r0, CHUNK), :]
        cc = c_s[pl.ds(r0, CHUNK), :]
        xdt = x * dte
        cb = _mm_nt(cc, bc)
        st = st_ref[...]
        y = _mm(cc, st) * jnp.exp(cs_e)
        tiles = []
        for t in range(hs // 2):
            xt = xdt[:, t * LANES:(t + 1) * LANES]
            ys = []
            for hh in (2 * t, 2 * t + 1):
                seg = cs_e[:, hh * SSM_HEAD_DIM:hh * SSM_HEAD_DIM + 1] - cs_r[d0 + hh:d0 + hh + 1, :]
                lmat = jnp.exp(jnp.where(tri, seg, -jnp.inf))
                ys.append(_mm(cb * lmat, xt))
            tiles.append(jnp.where(lane < SSM_HEAD_DIM, ys[0], ys[1]))
        y_ref[pl.ds(r0, CHUNK), :] = y + jnp.concatenate(tiles, axis=1)
        xw = xdt * jnp.exp(tot - cs_e)
        st_ref[...] = st * jnp.exp(tot) + _mm(bc.T, xw)

    def step(i):
        scan_chunk(i, True, stf_s, yf_s)
        scan_chunk(nc - 1 - i, False, stb_s, yb_s)

    def body(j, carry):
        for u in range(SSD_UNROLL):
            step(j * SSD_UNROLL + u)
        return carry

    lax.fori_loop(0, nc // SSD_UNROLL, body, 0)
    for i in range(nc - nc % SSD_UNROLL, nc):
        step(i)

    d_skip = _expand_heads(pc[2:3], 0, hs)

    def fin(c, carry):
        r0 = pl.multiple_of(c * CHUNK, CHUNK)
        y = (yf_s[pl.ds(r0, CHUNK), :] + yb_s[pl.ds(r0, CHUNK), :]) + d_skip * xs_s[pl.ds(r0, CHUNK), :]
        zc = z_ref[0, pl.ds(r0, CHUNK), :]
        out_ref[0, pl.ds(pl.multiple_of(r0 - CHUNK, CHUNK), CHUNK), :] = y * (zc * _sigmoid(zc))
        return carry

    lax.fori_loop(1, nc, fin, 0)


def _ssd(z, xbc, dt_col, dt_row, w):
    b, lp, _ = z.shape
    nc = lp // CHUNK
    s = lp - CHUNK
    hs = SSD_HEADS_PER_STEP
    ng = SSM_HEADS // hs
    cw = hs * SSM_HEAD_DIM
    per_group = ng // SSM_GROUPS
    b_blk = D_SSM // D_STATE
    c_blk = b_blk + SSM_GROUPS
    x_map = lambda i, g: (i, 0, g)
    b_map = lambda i, g: (i, 0, b_blk + g // per_group)
    c_map = lambda i, g: (i, 0, c_blk + g // per_group)
    wx_map = lambda i, g: (0, g)
    wb_map = lambda i, g: (0, b_blk + g // per_group)
    wc_map = lambda i, g: (0, c_blk + g // per_group)
    return pl.pallas_call(
        functools.partial(_ssd_kernel, hs=hs, nc=nc),
        grid=(b, ng),
        in_specs=[pl.BlockSpec((1, lp, cw), x_map), pl.BlockSpec((1, lp, D_STATE), b_map),
                  pl.BlockSpec((1, lp, D_STATE), c_map), pl.BlockSpec((1, lp, cw), x_map),
                  pl.BlockSpec((D_CONV, cw), wx_map), pl.BlockSpec((1, cw), wx_map),
                  pl.BlockSpec((D_CONV, D_STATE), wb_map), pl.BlockSpec((1, D_STATE), wb_map),
                  pl.BlockSpec((D_CONV, D_STATE), wc_map), pl.BlockSpec((1, D_STATE), wc_map),
                  pl.BlockSpec((1, 1, lp, LANES), lambda i, g: (i, g, 0, 0)),
                  pl.BlockSpec((1, 1, 2 * hs, lp), lambda i, g: (i, g, 0, 0)),
                  pl.BlockSpec((1, 3, LANES), lambda i, g: (g, 0, 0)),
                  pl.BlockSpec((1, 2 * hs, 2), lambda i, g: (g, 0, 0))],
        out_specs=pl.BlockSpec((1, s, cw), x_map),
        out_shape=jax.ShapeDtypeStruct((b, s, D_SSM), F32),
        scratch_shapes=[pltpu.VMEM((lp, cw), F32), pltpu.VMEM((lp, D_STATE), F32),
                        pltpu.VMEM((lp, D_STATE), F32), pltpu.VMEM((lp, cw), F32), pltpu.VMEM((lp, cw), F32),
                        pltpu.VMEM((D_STATE, cw), F32), pltpu.VMEM((D_STATE, cw), F32),
                        pltpu.VMEM((lp, LANES), F32),
                        pltpu.VMEM((2 * hs, lp), F32)],
        compiler_params=_cparams(2),
        name="ssd",
    )(xbc, xbc, xbc, z, w["conv_w"], w["conv_b"], w["conv_w"], w["conv_b"], w["conv_w"], w["conv_b"],
      dt_col, dt_row, w["ssd_pcol"], w["ssd_prow"])


def _attn_kernel(q_ref, k_ref, v_ref, bias_ref, o_ref, *, tq):
    i = pl.program_id(2)
    rows = tq // ATTN_SPLIT
    for part in range(ATTN_SPLIT):
        r0 = pl.multiple_of(CHUNK + i * tq + part * rows, CHUNK)
        q = q_ref[0, 0, pl.ds(r0, rows), :]
        s = lax.dot_general(q, k_ref[0, 0], (((1,), (1,)), ((), ())), preferred_element_type=F32)
        s = s + bias_ref[...]
        m = jnp.max(s, axis=-1, keepdims=True)
        p = jnp.exp(s - m)
        l = jnp.sum(p, axis=-1, keepdims=True)
        o = jnp.dot(p.astype(BF16), v_ref[0, 0], preferred_element_type=F32)
        o_ref[0, part * rows:(part + 1) * rows, :] = o / l


def _attention(q, k, v, key_bias):
    b, nh, lp, _ = q.shape
    s = lp - CHUNK
    tq = next(t for t in (512, 256, 128) if s % t == 0)
    kv_spec = pl.BlockSpec((1, 1, lp, HEAD_PAD), lambda i, h, j: (i, h, 0, 0))
    return pl.pallas_call(
        functools.partial(_attn_kernel, tq=tq),
        grid=(b, nh, s // tq),
        in_specs=[kv_spec, kv_spec, kv_spec, _full((1, lp))],
        out_specs=pl.BlockSpec((1, tq, HEAD_PAD), lambda i, h, j: (i, j, h)),
        out_shape=jax.ShapeDtypeStruct((b, s, D_HEADS_PAD), F32),
        compiler_params=_cparams(3),
        name="attention",
    )(q, k, v, key_bias)


def _outproj_kernel(x_ref, y_ref, o_ref, lng_ref, lnb_ref, sg_ref, ag_ref, w1_ref, w2_ref,
                    g1_ref, b1_ref, h1_ref):
    h0 = _ln(x_ref[0], lng_ref[...], lnb_ref[...])
    ysn = _rms(y_ref[0], sg_ref[...], D_SSM)
    on = _rms(o_ref[0], ag_ref[...], D_ATTN)
    mix = _mm(ysn, w1_ref[...]) + _mm(on, w2_ref[...])
    h1_ref[0] = _ln(DEEPNORM_ALPHA * h0 + mix, g1_ref[...], b1_ref[...])


def _outproj(x, yg, o, w):
    b, s, _ = x.shape
    tb = next(t for t in (512, 256, 128) if s % t == 0)
    blk = lambda width: pl.BlockSpec((1, tb, width), lambda i, j: (i, j, 0))
    vec = lambda width: _full((1, width))
    return pl.pallas_call(
        _outproj_kernel,
        grid=(b, s // tb),
        in_specs=[blk(D_MODEL), blk(D_SSM), blk(D_HEADS_PAD), vec(D_MODEL), vec(D_MODEL), vec(D_SSM),
                  vec(D_HEADS_PAD), _full((D_SSM, D_MODEL)), _full((D_HEADS_PAD, D_MODEL)),
                  vec(D_MODEL), vec(D_MODEL)],
        out_specs=blk(D_MODEL),
        out_shape=jax.ShapeDtypeStruct((b, s, D_MODEL), F32),
        compiler_params=_cparams(2),
        name="outproj",
    )(x, yg, o, w["ln_in_g"], w["ln_in_b"], w["ssm_norm_g"], w["attn_norm_g"], w["w_out_ssm"],
      w["w_out_attn"], w["ln1_g"], w["ln1_b"])


def _topk_rows(s, vals_s, idx_s, row0, payload=None):
    n = s.shape[0]
    iota = lax.broadcasted_iota(I32, s.shape, 0)
    for r in range(PEER_TOPK):
        m = jnp.max(s, axis=0, keepdims=True)
        win = jnp.min(jnp.where(s == m, iota, n), axis=0, keepdims=True)
        hit = iota == win
        vals_s[row0 + r:row0 + r + 1, :] = m
        if payload is None:
            idx_s[row0 + r:row0 + r + 1, :] = win
        else:
            idx_s[row0 + r:row0 + r + 1, :] = jnp.sum(jnp.where(hit, payload, 0), axis=0, keepdims=True)
        s = jnp.where(hit, -jnp.inf, s)


_CAND_ROWS = [PEER_TOPK // (i + 1) for i in range(PEER_TOPK)]
_N_CAND = sum(_CAND_ROWS)
_N_CAND_PAD = -(-_N_CAND // SUBLANES) * SUBLANES


def _route_kernel(h_ref, wq_ref, keys_ref, idx_ref, g_ref,
                  v1_s, i1_s, v2_s, i2_s, cand_s, cidx_s, top_s, exp_s, gate_s, *, tb):
    q = _mm(h_ref[...], wq_ref[...]).astype(BF16)
    cand_s[...] = jnp.full((_N_CAND_PAD, tb), -jnp.inf, F32)
    cidx_s[...] = jnp.zeros((_N_CAND_PAD, tb), I32)
    for hd in range(PEER_HEADS):
        q1 = q[:, hd * D_KEY:hd * D_KEY + D_SUBKEY]
        q2 = q[:, hd * D_KEY + D_SUBKEY:(hd + 1) * D_KEY]
        s1 = lax.dot_general(keys_ref[0, hd], q1, (((1,), (1,)), ((), ())), preferred_element_type=F32)
        s2 = lax.dot_general(keys_ref[1, hd], q2, (((1,), (1,)), ((), ())), preferred_element_type=F32)
        _topk_rows(s1, v1_s, i1_s, 0)
        _topk_rows(s2, v2_s, i2_s, 0)
        off = 0
        for i, n in enumerate(_CAND_ROWS):
            cand_s[off:off + n, :] = v1_s[i:i + 1, :] + v2_s[0:n, :]
            cidx_s[off:off + n, :] = i1_s[i:i + 1, :] * N_KEYS + i2_s[0:n, :]
            off += n
        _topk_rows(cand_s[...], top_s, exp_s, hd * PEER_TOPK, payload=cidx_s[...])
        ts = top_s[hd * PEER_TOPK:(hd + 1) * PEER_TOPK, :]
        e = jnp.exp(ts - jnp.max(ts, axis=0, keepdims=True))
        gate_s[hd * PEER_TOPK:(hd + 1) * PEER_TOPK, :] = e / jnp.sum(e, axis=0, keepdims=True)
    idx_ref[...] = (exp_s[...] * SLAB).T
    g_ref[...] = gate_s[...].T


def _route(h1, w):
    t = h1.shape[0]
    tb = 256 if t % 256 == 0 else 128
    return pl.pallas_call(
        functools.partial(_route_kernel, tb=tb),
        grid=(t // tb,),
        in_specs=[pl.BlockSpec((tb, D_MODEL), lambda i: (i, 0)),
                  _full((D_MODEL, PEER_HEADS * D_KEY)),
                  _full((2, PEER_HEADS, N_KEYS, D_SUBKEY))],
        out_specs=[pl.BlockSpec((tb, PAIRS), lambda i: (i, 0)), pl.BlockSpec((tb, PAIRS), lambda i: (i, 0))],
        out_shape=[jax.ShapeDtypeStruct((t, PAIRS), I32), jax.ShapeDtypeStruct((t, PAIRS), F32)],
        scratch_shapes=[pltpu.VMEM((PEER_TOPK, tb), F32), pltpu.VMEM((PEER_TOPK, tb), I32),
                        pltpu.VMEM((PEER_TOPK, tb), F32), pltpu.VMEM((PEER_TOPK, tb), I32),
                        pltpu.VMEM((_N_CAND_PAD, tb), F32), pltpu.VMEM((_N_CAND_PAD, tb), I32),
                        pltpu.VMEM((PAIRS, tb), F32), pltpu.VMEM((PAIRS, tb), I32),
                        pltpu.VMEM((PAIRS, tb), F32)],
        compiler_params=_cparams(1),
        name="peer_route",
    )(h1, w["peer_w_query"], w["peer_sub_keys"])


def _rows_f32(slab):
    return pltpu.bitcast(slab, BF16).astype(F32)


def _gelu(x):
    return 0.5 * x * (1.0 + lax.erf(x * (1.0 / math.sqrt(2.0))))


def _peer_u_kernel(idx_ref, x_ref, g_ref, tab_ref, w_ref, prod_a, prod_b, act_s, *, tb):
    ones = jnp.ones((SUBLANES, LANES), BF16)
    ri = lax.broadcasted_iota(I32, (PAIR_ROWS, PAIRS), 0)
    ci = lax.broadcasted_iota(I32, (PAIR_ROWS, PAIRS), 1)
    group = (lax.shift_right_logical(ri, 2) == ci).astype(BF16)
    sub = lax.broadcasted_iota(I32, (SUBLANES, PAIR_ROWS), 0)

    def gather(t0, dst):
        for n in range(U_PHASE):
            t = t0 + n
            xt = x_ref[pl.ds(pl.multiple_of(t * SUBLANES, SUBLANES), SUBLANES), :]
            base = t * PAIRS
            for k in range(PAIRS):
                row = pl.multiple_of(idx_ref[base + k], SLAB)
                p = _rows_f32(tab_ref[pl.ds(row, SLAB), :]) * xt
                dst[n * PAIR_ROWS + k * SLAB:n * PAIR_ROWS + (k + 1) * SLAB, :] = p[0:SLAB] + p[SLAB:2 * SLAB]

    def reduce(src, t0):
        sums = lax.dot_general(ones, src[...].astype(BF16), (((1,), (1,)), ((), ())),
                               preferred_element_type=F32)
        per_tok = sums[:, 0:PAIR_ROWS]
        for n in range(1, U_PHASE):
            per_tok = jnp.where(sub == n, sums[:, n * PAIR_ROWS:(n + 1) * PAIR_ROWS], per_tok)
        act_s[pl.ds(pl.multiple_of(t0, U_PHASE), U_PHASE), :] = jnp.dot(
            per_tok.astype(BF16), group, preferred_element_type=F32)

    prod_b[...] = jnp.zeros((U_PHASE * PAIR_ROWS, LANES), F32)

    def body(j, carry):
        t0 = 2 * U_PHASE * j
        gather(t0, prod_a)
        reduce(prod_b, jnp.maximum(t0 - U_PHASE, 0))
        gather(t0 + U_PHASE, prod_b)
        reduce(prod_a, t0)
        return carry

    lax.fori_loop(0, tb // (2 * U_PHASE), body, 0)
    reduce(prod_b, tb - U_PHASE)
    w_ref[...] = g_ref[...] * _gelu(act_s[...])


def _peer_v_kernel(idx_ref, wt_ref, tab_ref, out_ref, wt_s, wb_a, wb_b, *, tb):
    n_acc = 4
    wt_s[...] = wt_ref[...].T
    lane = lax.broadcasted_iota(I32, (PAIRS, tb), 1)

    def spread(t, dst):
        col = jnp.sum(jnp.where(lane == t, wt_s[...], 0.0), axis=1, keepdims=True)
        dst[...] = jnp.broadcast_to(col, (PAIRS, LANES))

    def token(t, wb):
        base = t * PAIRS
        acc = [jnp.zeros((SUBLANES, LANES), F32) for _ in range(n_acc)]
        for k in range(PAIRS):
            row = pl.multiple_of(idx_ref[base + k], SLAB)
            acc[k % n_acc] = acc[k % n_acc] + wb[k:k + 1, :] * _rows_f32(tab_ref[pl.ds(row, SLAB), :])
        out_ref[pl.ds(pl.multiple_of(t * SUBLANES, SUBLANES), SUBLANES), :] = (acc[0] + acc[1]) + (acc[2] + acc[3])

    spread(0, wb_a)

    def body(j, carry):
        t0 = 2 * j
        spread(t0 + 1, wb_b)
        token(t0, wb_a)
        spread(jnp.minimum(t0 + 2, tb - 1), wb_a)
        token(t0 + 1, wb_b)
        return carry

    lax.fori_loop(0, tb // 2, body, 0)


def _peer_tb(t):
    return next(c for c in (128, 64, 32, 16) if t % c == 0)


def _table_spec(n_rows):
    return pl.BlockSpec((n_rows, LANES), lambda i: (0, 0), pipeline_mode=pl.Buffered(1))


def _peer_u(idx_flat, x8, gates, table):
    t = gates.shape[0]
    tb = _peer_tb(t)
    return pl.pallas_call(
        functools.partial(_peer_u_kernel, tb=tb),
        grid=(t // tb,),
        in_specs=[pl.BlockSpec((tb * PAIRS,), lambda i: (i,), memory_space=pltpu.SMEM),
                  pl.BlockSpec((tb * SUBLANES, LANES), lambda i: (i, 0)),
                  pl.BlockSpec((tb, PAIRS), lambda i: (i, 0)),
                  _table_spec(table.shape[0])],
        out_specs=pl.BlockSpec((tb, PAIRS), lambda i: (i, 0)),
        out_shape=jax.ShapeDtypeStruct((t, PAIRS), F32),
        scratch_shapes=[pltpu.VMEM((U_PHASE * PAIR_ROWS, LANES), F32),
                        pltpu.VMEM((U_PHASE * PAIR_ROWS, LANES), F32), pltpu.VMEM((tb, PAIRS), F32)],
        compiler_params=_cparams(1),
        name="peer_u",
    )(idx_flat, x8, gates, table)


def _peer_v(idx_flat, wt, table):
    t = wt.shape[0]
    tb = _peer_tb(t)
    return pl.pallas_call(
        functools.partial(_peer_v_kernel, tb=tb),
        grid=(t // tb,),
        in_specs=[pl.BlockSpec((tb * PAIRS,), lambda i: (i,), memory_space=pltpu.SMEM),
                  pl.BlockSpec((tb, PAIRS), lambda i: (i, 0)),
                  _table_spec(table.shape[0])],
        out_specs=pl.BlockSpec((tb * SUBLANES, LANES), lambda i: (i, 0)),
        out_shape=jax.ShapeDtypeStruct((t * SUBLANES, LANES), F32),
        scratch_shapes=[pltpu.VMEM((PAIRS, tb), F32), pltpu.VMEM((PAIRS, LANES), F32),
                        pltpu.VMEM((PAIRS, LANES), F32)],
        compiler_params=_cparams(1),
        name="peer_v",
    )(idx_flat, wt, table)


def _ln2_kernel(h_ref, p_ref, g_ref, b_ref, o_ref):
    o_ref[...] = _ln(DEEPNORM_ALPHA * h_ref[...] + p_ref[...], g_ref[...], b_ref[...])


def _ln2(h1, peer, w):
    t = h1.shape[0]
    tb = next(c for c in (512, 256, 128) if t % c == 0)
    blk = pl.BlockSpec((tb, D_MODEL), lambda i: (i, 0))
    return pl.pallas_call(
        _ln2_kernel,
        grid=(t // tb,),
        in_specs=[blk, blk, _full((1, D_MODEL)), _full((1, D_MODEL))],
        out_specs=blk,
        out_shape=jax.ShapeDtypeStruct((t, D_MODEL), F32),
        compiler_params=_cparams(1),
        name="ln2",
    )(h1, peer, w["ln2_g"], w["ln2_b"])


def _pack_table(tab):
    e = tab.shape[0]
    bits = lax.bitcast_convert_type(tab.astype(BF16), jnp.uint16).astype(U32).reshape(e, SLAB, 2, LANES)
    return (bits[:, :, 0] | (bits[:, :, 1] << 16)).reshape(e * SLAB, LANES)


def _rot_cols(wr):
    half = QK_ROPE_DIM // 2
    return jnp.concatenate([-wr[..., half:], wr[..., :half]], axis=-1)


def _prep_weights(ln_in_g, ln_in_b, w_in, conv_w, conv_b, dt_bias_fwd, dt_bias_bwd, a_log_fwd, a_log_bwd,
                  d_skip, ssm_norm_g, q_norm_g, w_uq, kv_norm_g, w_ukv, attn_norm_g, w_out, ln1_g, ln1_b,
                  peer_w_query, peer_sub_keys, peer_u, peer_v, ln2_g, ln2_b):
    li = 0
    row = lambda v: v.reshape(1, -1).astype(F32)
    w = {}
    w["ln_in_g"], w["ln_in_b"] = row(ln_in_g), row(ln_in_b)
    wi = w_in[li]
    w_kr = wi[:, SPLIT_CKV:]
    zc = lambda n: jnp.zeros((D_MODEL, n), F32)
    tail_a = jnp.concatenate([wi[:, SPLIT_XBC:SPLIT_DT], w_kr, zc(LANES - KR_LANE - QK_ROPE_DIM)], axis=1)
    tail_b = jnp.concatenate([zc(KR_LANE), _rot_cols(w_kr), zc(LANES - KR_LANE - QK_ROPE_DIM)], axis=1)
    w["w_proj"] = jnp.concatenate([wi[:, :SPLIT_XBC], wi[:, SPLIT_DT:SPLIT_CKV], tail_a, tail_b],
                                  axis=1).astype(BF16)
    w["conv_w"], w["conv_b"] = conv_w[li].astype(F32), row(conv_b[li])
    hs = SSD_HEADS_PER_STEP
    ng = SSM_HEADS // hs
    per_dir = lambda f, b_: jnp.concatenate([f[li].reshape(ng, hs), b_[li].reshape(ng, hs)], axis=1)
    dtb, alog = per_dir(dt_bias_fwd, dt_bias_bwd), per_dir(a_log_fwd, a_log_bwd)
    dsk = per_dir(d_skip, d_skip)
    pcol = jnp.stack([dtb, alog, dsk], axis=1)
    w["ssd_pcol"] = jnp.pad(pcol, ((0, 0), (0, 0), (0, LANES - 2 * hs))).astype(F32)
    w["ssd_prow"] = jnp.stack([dtb, alog], axis=2).astype(F32)
    w["ssm_norm_g"] = row(ssm_norm_g[li])
    w["q_norm_g"], w["kv_norm_g"] = row(q_norm_g[li]), row(kv_norm_g[li])
    pad_h = lambda a, n: jnp.pad(a, ((0, 0), (0, 0), (0, HEAD_PAD - n)))
    uq = w_uq[li].reshape(Q_LORA_RANK, MLA_HEADS, QK_NOPE_DIM + QK_ROPE_DIM)
    uq_rot = jnp.concatenate([jnp.zeros_like(uq[..., :QK_NOPE_DIM]), _rot_cols(uq[..., QK_NOPE_DIM:])], axis=-1)
    w["w_uq"] = pad_h(uq, QK_NOPE_DIM + QK_ROPE_DIM).reshape(Q_LORA_RANK, D_HEADS_PAD).astype(BF16)
    w["w_uq_rot"] = pad_h(uq_rot, QK_NOPE_DIM + QK_ROPE_DIM).reshape(Q_LORA_RANK, D_HEADS_PAD).astype(BF16)
    ukv = w_ukv[li].reshape(KV_LORA_RANK, MLA_HEADS, QK_NOPE_DIM + V_HEAD_DIM)
    w["w_uk"] = pad_h(ukv[..., :QK_NOPE_DIM], QK_NOPE_DIM).reshape(KV_LORA_RANK, D_HEADS_PAD).astype(BF16)
    w["w_uv"] = pad_h(ukv[..., QK_NOPE_DIM:], V_HEAD_DIM).reshape(KV_LORA_RANK, D_HEADS_PAD).astype(BF16)
    src = KR_LANE + jnp.arange(QK_ROPE_DIM)
    e_kr = jnp.zeros((LANES, MLA_HEADS, HEAD_PAD), F32)
    e_kr = e_kr.at[src, :, QK_NOPE_DIM + jnp.arange(QK_ROPE_DIM)].set(1.0)
    w["e_kr"] = e_kr.reshape(LANES, D_HEADS_PAD).astype(BF16)
    w["attn_norm_g"] = pad_h(attn_norm_g[li].reshape(1, MLA_HEADS, V_HEAD_DIM), V_HEAD_DIM).reshape(1, D_HEADS_PAD)
    wo = w_out[li]
    w["w_out_ssm"] = wo[:D_SSM].astype(BF16)
    wo_attn = wo[D_SSM:].reshape(MLA_HEADS, V_HEAD_DIM, D_MODEL)
    w["w_out_attn"] = jnp.pad(wo_attn, ((0, 0), (0, HEAD_PAD - V_HEAD_DIM), (0, 0))).reshape(
        D_HEADS_PAD, D_MODEL).astype(BF16)
    w["ln1_g"], w["ln1_b"] = row(ln1_g[li]), row(ln1_b[li])
    w["peer_w_query"] = peer_w_query[li].astype(BF16)
    w["peer_sub_keys"] = peer_sub_keys[li].astype(BF16)
    w["u_table"] = _pack_table(peer_u[li])
    w["v_table"] = _pack_table(peer_v[li])
    w["ln2_g"], w["ln2_b"] = row(ln2_g[li]), row(ln2_b[li])
    return w


def _rope_tables(lp):
    half = QK_ROPE_DIM // 2
    pos = jnp.arange(lp, dtype=F32) - float(META_PAD)
    inv = ROPE_THETA ** (-jnp.arange(half, dtype=F32) / half)
    ang = pos[:, None] * inv[None, :]
    cos2 = jnp.tile(jnp.cos(ang), (1, 2))
    sin2 = jnp.tile(jnp.sin(ang), (1, 2))
    scale = 1.0 / math.sqrt(QK_NOPE_DIM + QK_ROPE_DIM)
    tail = HEAD_PAD - QK_NOPE_DIM - QK_ROPE_DIM
    cq = jnp.concatenate([jnp.ones((lp, QK_NOPE_DIM), F32), cos2, jnp.zeros((lp, tail), F32)], axis=1)
    sq = jnp.concatenate([jnp.zeros((lp, QK_NOPE_DIM), F32), sin2, jnp.zeros((lp, tail), F32)], axis=1)
    place = lambda a: jnp.pad(a, ((0, 0), (KR_LANE, LANES - KR_LANE - QK_ROPE_DIM)))
    key_bias = jnp.where(jnp.arange(lp) >= META_PAD, 0.0, -jnp.inf).astype(F32).reshape(1, lp)
    return {"cos_q": jnp.tile(cq, (1, MLA_HEADS)) * scale, "sin_q": jnp.tile(sq, (1, MLA_HEADS)) * scale,
            "cos_k": place(cos2), "sin_k": place(sin2), "key_bias": key_bias}


def _encode(x, meta_tokens, w):
    b, s, _ = x.shape
    lp = s + CHUNK
    lead = jnp.concatenate([jnp.zeros((META_PAD, D_MODEL), F32), meta_tokens.astype(F32)], axis=0)
    xpad = jnp.concatenate([jnp.broadcast_to(lead[None], (b, CHUNK, D_MODEL)), x], axis=1)
    tabs = _rope_tables(lp)
    z, xbc, tail, q, k, v = _inproj(xpad, tabs, w)
    hs = SSD_HEADS_PER_STEP
    ng = SSM_HEADS // hs
    dt = tail[..., :2 * SSM_HEADS].reshape(b, lp, 2, ng, hs)
    dt = jnp.transpose(dt, (0, 3, 1, 2, 4)).reshape(b, ng, lp, 2 * hs)
    dt_col = jnp.pad(dt, ((0, 0), (0, 0), (0, 0), (0, LANES - 2 * hs)))
    dt_row = jnp.swapaxes(dt, 2, 3)
    yg = _ssd(z, xbc, dt_col, dt_row, w)
    o = _attention(q, k, v, tabs["key_bias"])
    h1 = _outproj(x, yg, o, w).reshape(b * s, D_MODEL)
    idx, gates = _route(h1, w)
    idx_flat = idx.reshape(-1)
    wt = _peer_u(idx_flat, h1.reshape(b * s * SUBLANES, LANES), gates, w["u_table"])
    peer = _peer_v(idx_flat, wt, w["v_table"])
    return _ln2(h1, peer.reshape(b * s, D_MODEL), w).reshape(b, s, D_MODEL)


def kernel(x_prompt, x_sample, meta_tokens, ln_in_g, ln_in_b, w_in, conv_w, conv_b, dt_bias_fwd, dt_bias_bwd,
           a_log_fwd, a_log_bwd, d_skip, ssm_norm_g, q_norm_g, w_uq, kv_norm_g, w_ukv, attn_norm_g, w_out,
           ln1_g, ln1_b, peer_w_query, peer_sub_keys, peer_u, peer_v, ln2_g, ln2_b):
    w = _prep_weights(ln_in_g, ln_in_b, w_in, conv_w, conv_b, dt_bias_fwd, dt_bias_bwd, a_log_fwd, a_log_bwd,
                      d_skip, ssm_norm_g, q_norm_g, w_uq, kv_norm_g, w_ukv, attn_norm_g, w_out, ln1_g, ln1_b,
                      peer_w_query, peer_sub_keys, peer_u, peer_v, ln2_g, ln2_b)
    return (_encode(x_prompt, meta_tokens, w), _encode(x_sample, meta_tokens, w))
```

```python
import functools
import math

import jax
import jax.numpy as jnp
from jax import lax
from jax.experimental import pallas as pl
from jax.experimental.pallas import tpu as pltpu

F32 = jnp.float32
BF16 = jnp.bfloat16
I32 = jnp.int32
U32 = jnp.uint32

D_MODEL = 1024
N_META = 16
SSM_HEADS = 16
SSM_HEAD_DIM = 64
D_SSM = SSM_HEADS * SSM_HEAD_DIM
SSM_GROUPS = 2
D_STATE = 128
D_CONV = 5
D_CONV_CH = D_SSM + 2 * SSM_GROUPS * D_STATE
CHUNK = 128
META_PAD = CHUNK - N_META
MLA_HEADS = 8
QK_NOPE_DIM = 64
QK_ROPE_DIM = 32
V_HEAD_DIM = 64
Q_LORA_RANK = 384
KV_LORA_RANK = 256
D_ATTN = MLA_HEADS * V_HEAD_DIM
ROPE_THETA = 10000.0
SPLIT_Z = D_SSM
SPLIT_XBC = SPLIT_Z + D_CONV_CH
SPLIT_DT = SPLIT_XBC + 2 * SSM_HEADS
SPLIT_CQ = SPLIT_DT + Q_LORA_RANK
SPLIT_CKV = SPLIT_CQ + KV_LORA_RANK
PEER_HEADS = 8
N_KEYS = 128
PEER_TOPK = 16
D_KEY = 256
D_SUBKEY = D_KEY // 2
DEPTH = 1
DEEPNORM_ALPHA = (2.0 * DEPTH) ** 0.25
EPS = 1e-5

LANES = 128
SUBLANES = 8
HEAD_PAD = LANES
D_HEADS_PAD = MLA_HEADS * HEAD_PAD
VMEM_LIMIT = 56 * 1024 * 1024
SSD_HEADS_PER_STEP = 8
SSD_UNROLL = 2
ATTN_ROWS = 256
PAIRS = PEER_HEADS * PEER_TOPK
SLAB = 4
PAIR_ROWS = PAIRS * SLAB
U_PHASE = SUBLANES
V_PHASE = 4
V_STEP = 32

OFF_Z = 0
OFF_XBC = D_SSM
OFF_CQ = OFF_XBC + D_CONV_CH
OFF_CKV = OFF_CQ + Q_LORA_RANK
OFF_TAIL_A = OFF_CKV + KV_LORA_RANK
OFF_TAIL_B = OFF_TAIL_A + LANES
D_PROJ = OFF_TAIL_B + LANES
KR_LANE = 2 * SSM_HEADS


def _cparams(n_axes):
    return pltpu.CompilerParams(dimension_semantics=("arbitrary",) * n_axes,
                                vmem_limit_bytes=VMEM_LIMIT)


def _full(shape):
    zeros = (0,) * len(shape)
    return pl.BlockSpec(shape, lambda *_: zeros)


def _ln(x, g, b):
    mu = jnp.mean(x, axis=-1, keepdims=True)
    xc = x - mu
    var = jnp.mean(xc * xc, axis=-1, keepdims=True)
    return xc * lax.rsqrt(var + EPS) * g + b


def _rms(x, g, n):
    ms = jnp.sum(x * x, axis=-1, keepdims=True) * (1.0 / n)
    return x * lax.rsqrt(ms + EPS) * g


def _mm(a, b):
    return jnp.dot(a.astype(BF16), b.astype(BF16), preferred_element_type=F32)


def _mm_nt(a, b):
    return lax.dot_general(a.astype(BF16), b.astype(BF16), (((1,), (1,)), ((), ())),
                           preferred_element_type=F32)


def _mm_exact(a, b):
    return jnp.dot(a, b, preferred_element_type=F32, precision=lax.Precision.HIGHEST)


def _sigmoid(x):
    return 1.0 / (1.0 + jnp.exp(-x))


def _softplus(x):
    return jnp.maximum(x, 0.0) + jnp.log(1.0 + jnp.exp(-jnp.abs(x)))


def _inproj_kernel(x_ref, cq_ref, sq_ref, ck_ref, sk_ref, lng_ref, lnb_ref, win_ref, qng_ref,
                   wuq_ref, wuqr_ref, kvng_ref, wuk_ref, wuv_ref, ekr_ref,
                   z_ref, xbc_ref, tail_ref, q_ref, k_ref, v_ref, *, tm):
    j = pl.program_id(1)
    h = _ln(x_ref[0], lng_ref[...], lnb_ref[...])
    proj = _mm(h, win_ref[...])
    row = j * tm + lax.broadcasted_iota(I32, (tm, 1), 0)
    valid = row >= META_PAD
    z_ref[0] = proj[:, OFF_Z:OFF_XBC]
    xbc_ref[0] = jnp.where(valid, proj[:, OFF_XBC:OFF_CQ], 0.0)
    tail = proj[:, OFF_TAIL_A:OFF_TAIL_B]
    tail_ref[0] = tail
    cqn = _rms(proj[:, OFF_CQ:OFF_CKV], qng_ref[...], Q_LORA_RANK)
    qf = _mm(cqn, wuq_ref[...]) * cq_ref[...] + _mm(cqn, wuqr_ref[...]) * sq_ref[...]
    ckvn = _rms(proj[:, OFF_CKV:OFF_TAIL_A], kvng_ref[...], KV_LORA_RANK)
    kt = tail * ck_ref[...] + proj[:, OFF_TAIL_B:D_PROJ] * sk_ref[...]
    kf = _mm(ckvn, wuk_ref[...]) + _mm(kt, ekr_ref[...])
    vf = _mm(ckvn, wuv_ref[...])
    for hd in range(MLA_HEADS):
        sl = slice(hd * HEAD_PAD, (hd + 1) * HEAD_PAD)
        q_ref[0, hd] = qf[:, sl].astype(BF16)
        k_ref[0, hd] = kf[:, sl].astype(BF16)
        v_ref[0, hd] = vf[:, sl].astype(BF16)


def _inproj(xpad, tabs, w):
    b, lp, _ = xpad.shape
    tm = next(t for t in (544, 384, 256, 128) if lp % t == 0)
    grid = (b, lp // tm)
    row_spec = lambda width: pl.BlockSpec((tm, width), lambda i, j: (j, 0))
    seq_spec = lambda width: pl.BlockSpec((1, tm, width), lambda i, j: (i, j, 0))
    head_spec = pl.BlockSpec((1, MLA_HEADS, tm, HEAD_PAD), lambda i, j: (i, 0, j, 0))
    head_shape = jax.ShapeDtypeStruct((b, MLA_HEADS, lp, HEAD_PAD), BF16)
    return pl.pallas_call(
        functools.partial(_inproj_kernel, tm=tm),
        grid=grid,
        in_specs=[seq_spec(D_MODEL), row_spec(D_HEADS_PAD), row_spec(D_HEADS_PAD), row_spec(LANES),
                  row_spec(LANES), _full((1, D_MODEL)), _full((1, D_MODEL)), _full((D_MODEL, D_PROJ)),
                  _full((1, Q_LORA_RANK)), _full((Q_LORA_RANK, D_HEADS_PAD)),
                  _full((Q_LORA_RANK, D_HEADS_PAD)), _full((1, KV_LORA_RANK)),
                  _full((KV_LORA_RANK, D_HEADS_PAD)), _full((KV_LORA_RANK, D_HEADS_PAD)),
                  _full((LANES, D_HEADS_PAD))],
        out_specs=[seq_spec(D_SSM), seq_spec(D_CONV_CH), seq_spec(LANES), head_spec, head_spec, head_spec],
        out_shape=[jax.ShapeDtypeStruct((b, lp, D_SSM), F32),
                   jax.ShapeDtypeStruct((b, lp, D_CONV_CH), F32),
                   jax.ShapeDtypeStruct((b, lp, LANES), F32),
                   head_shape, head_shape, head_shape],
        compiler_params=_cparams(2),
        name="inproj",
    )(xpad, tabs["cos_q"], tabs["sin_q"], tabs["cos_k"], tabs["sin_k"], w["ln_in_g"], w["ln_in_b"],
      w["w_proj"], w["q_norm_g"], w["w_uq"], w["w_uq_rot"], w["kv_norm_g"], w["w_uk"], w["w_uv"],
      w["e_kr"])


def _expand_heads(v, lane0, n_heads):
    lane = lax.broadcasted_iota(I32, (v.shape[0], LANES), 1)
    tiles = []
    for t in range(n_heads // 2):
        a = v[:, lane0 + 2 * t:lane0 + 2 * t + 1]
        b = v[:, lane0 + 2 * t + 1:lane0 + 2 * t + 2]
        tiles.append(jnp.where(lane < SSM_HEAD_DIM, a, b))
    return jnp.concatenate(tiles, axis=1)


def _conv_silu_chunk(win, w, bias, o, n):
    acc = bias + w[0:1] * win[o - 2:o - 2 + n]
    for k in range(1, D_CONV):
        acc = acc + w[k:k + 1] * win[o + k - 2:o + k - 2 + n]
    return acc * _sigmoid(acc)


def _conv_silu(src_ref, w_ref, b_ref, dst, nc):
    w = w_ref[...]
    bias = b_ref[...]
    lp = nc * CHUNK
    halo = SUBLANES
    width = src_ref.shape[-1]
    win = src_ref[0, 0:CHUNK + 2 * halo, :]
    dst[0:halo, :] = jnp.zeros((halo, width), F32)
    dst[halo:CHUNK, :] = _conv_silu_chunk(win, w, bias, halo, CHUNK - halo)

    def mid(c, carry):
        start = pl.multiple_of(c * CHUNK - halo, halo)
        win = src_ref[0, pl.ds(start, CHUNK + 2 * halo), :]
        dst[pl.ds(pl.multiple_of(c * CHUNK, CHUNK), CHUNK), :] = _conv_silu_chunk(win, w, bias, halo, CHUNK)
        return carry

    lax.fori_loop(1, nc - 1, mid, 0)
    win = jnp.concatenate([src_ref[0, lp - CHUNK - 2 * halo:lp, :], jnp.zeros((halo, width), F32)], axis=0)
    dst[lp - CHUNK:lp, :] = _conv_silu_chunk(win, w, bias, 2 * halo, CHUNK)


def _ssd_kernel(xs_ref, bm_ref, cm_ref, z_ref, cwx_ref, cbx_ref, cwb_ref, cbb_ref, cwc_ref, cbc_ref,
                dtc_ref, dtr_ref, pcol_ref, prow_ref, out_ref,
                xs_s, b_s, c_s, yf_s, yb_s, stf_s, stb_s, dtc_s, dtr_s, *, hs, nc):
    lp = nc * CHUNK
    cw = hs * SSM_HEAD_DIM
    pc = pcol_ref[0]
    pr = prow_ref[0]

    _conv_silu(xs_ref, cwx_ref, cbx_ref, xs_s, nc)
    _conv_silu(bm_ref, cwb_ref, cbb_ref, b_s, nc)
    _conv_silu(cm_ref, cwc_ref, cbc_ref, c_s, nc)

    rowi = lax.broadcasted_iota(I32, (lp, 1), 0)
    dtc_s[...] = jnp.where(rowi >= META_PAD, _softplus(dtc_ref[0, 0] + pc[0:1]), 0.0)
    lanei = lax.broadcasted_iota(I32, (1, lp), 1)
    dtr_s[...] = jnp.where(lanei >= META_PAD, _softplus(dtr_ref[0, 0] + pr[:, 0:1]), 0.0)
    a_col = -jnp.exp(pc[1:2])
    a_exp_f = _expand_heads(a_col, 0, hs)
    a_exp_b = _expand_heads(a_col, hs, hs)
    a_row = -jnp.exp(pr[:, 1:2])

    ri = lax.broadcasted_iota(I32, (CHUNK, CHUNK), 0)
    ci = lax.broadcasted_iota(I32, (CHUNK, CHUNK), 1)

    lane = lax.broadcasted_iota(I32, (CHUNK, LANES), 1)
    lower = ri >= ci
    upper = ri <= ci
    lower_f = lower.astype(F32)
    upper_f = upper.astype(F32)
    stf_s[...] = jnp.zeros((D_STATE, cw), F32)
    stb_s[...] = jnp.zeros((D_STATE, cw), F32)

    def scan_chunk(c, fwd, st_ref, y_ref):
        d0 = 0 if fwd else hs
        tri = lower if fwd else upper
        tmat = lower_f if fwd else upper_f
        umat = upper_f if fwd else lower_f
        r0 = pl.multiple_of(c * CHUNK, CHUNK)
        dte = _expand_heads(dtc_s[pl.ds(r0, CHUNK), :], d0, hs)
        cs_e = _mm_exact(tmat, dte * (a_exp_f if fwd else a_exp_b))
        a_r = dtr_s[:, pl.ds(r0, CHUNK)] * a_row
        cs_r = _mm_exact(a_r, umat)
        tot = cs_e[CHUNK - 1:CHUNK] if fwd else cs_e[0:1]
        x = xs_s[pl.ds(r0, CHUNK), :]
        bc = b_s[pl.ds(r0, CHUNK), :]
        cc = c_s[pl.ds(r0, CHUNK), :]
        xdt = x * dte
        cb = _mm_nt(cc, bc)
        st = st_ref[...]
        y = _mm(cc, st) * jnp.exp(cs_e)
        tiles = []
        for t in range(hs // 2):
            xt = xdt[:, t * LANES:(t + 1) * LANES]
            ys = []
            for hh in (2 * t, 2 * t + 1):
                seg = cs_e[:, hh * SSM_HEAD_DIM:hh * SSM_HEAD_DIM + 1] - cs_r[d0 + hh:d0 + hh + 1, :]
                lmat = jnp.exp(jnp.where(tri, seg, -jnp.inf))
                ys.append(_mm(cb * lmat, xt))
            tiles.append(jnp.where(lane < SSM_HEAD_DIM, ys[0], ys[1]))
        y_ref[pl.ds(r0, CHUNK), :] = y + jnp.concatenate(tiles, axis=1)
        xw = xdt * jnp.exp(tot - cs_e)
        st_ref[...] = st * jnp.exp(tot) + _mm(bc.T, xw)

    def step(i):
        scan_chunk(i, True, stf_s, yf_s)
        scan_chunk(nc - 1 - i, False, stb_s, yb_s)

    def body(j, carry):
        for u in range(SSD_UNROLL):
            step(j * SSD_UNROLL + u)
        return carry

    lax.fori_loop(0, nc // SSD_UNROLL, body, 0)
    for i in range(nc - nc % SSD_UNROLL, nc):
        step(i)

    d_skip = _expand_heads(pc[2:3], 0, hs)

    def fin(c, carry):
        r0 = pl.multiple_of(c * CHUNK, CHUNK)
        y = (yf_s[pl.ds(r0, CHUNK), :] + yb_s[pl.ds(r0, CHUNK), :]) + d_skip * xs_s[pl.ds(r0, CHUNK), :]
        zc = z_ref[0, pl.ds(r0, CHUNK), :]
        out_ref[0, pl.ds(pl.multiple_of(r0 - CHUNK, CHUNK), CHUNK), :] = y * (zc * _sigmoid(zc))
        return carry

    lax.fori_loop(1, nc, fin, 0)


def _ssd(z, xbc, dt_col, dt_row, w):
    b, lp, _ = z.shape
    nc = lp // CHUNK
    s = lp - CHUNK
    hs = SSD_HEADS_PER_STEP
    ng = SSM_HEADS // hs
    cw = hs * SSM_HEAD_DIM
    per_group = ng // SSM_GROUPS
    b_blk = D_SSM // D_STATE
    c_blk = b_blk + SSM_GROUPS
    x_map = lambda i, g: (i, 0, g)
    b_map = lambda i, g: (i, 0, b_blk + g // per_group)
    c_map = lambda i, g: (i, 0, c_blk + g // per_group)
    wx_map = lambda i, g: (0, g)
    wb_map = lambda i, g: (0, b_blk + g // per_group)
    wc_map = lambda i, g: (0, c_blk + g // per_group)
    return pl.pallas_call(
        functools.partial(_ssd_kernel, hs=hs, nc=nc),
        grid=(b, ng),
        in_specs=[pl.BlockSpec((1, lp, cw), x_map), pl.BlockSpec((1, lp, D_STATE), b_map),
                  pl.BlockSpec((1, lp, D_STATE), c_map), pl.BlockSpec((1, lp, cw), x_map),
                  pl.BlockSpec((D_CONV, cw), wx_map), pl.BlockSpec((1, cw), wx_map),
                  pl.BlockSpec((D_CONV, D_STATE), wb_map), pl.BlockSpec((1, D_STATE), wb_map),
                  pl.BlockSpec((D_CONV, D_STATE), wc_map), pl.BlockSpec((1, D_STATE), wc_map),
                  pl.BlockSpec((1, 1, lp, LANES), lambda i, g: (i, g, 0, 0)),
                  pl.BlockSpec((1, 1, 2 * hs, lp), lambda i, g: (i, g, 0, 0)),
                  pl.BlockSpec((1, 3, LANES), lambda i, g: (g, 0, 0)),
                  pl.BlockSpec((1, 2 * hs, 2), lambda i, g: (g, 0, 0))],
        out_specs=pl.BlockSpec((1, s, cw), x_map),
        out_shape=jax.ShapeDtypeStruct((b, s, D_SSM), F32),
        scratch_shapes=[pltpu.VMEM((lp, cw), F32), pltpu.VMEM((lp, D_STATE), F32),
                        pltpu.VMEM((lp, D_STATE), F32), pltpu.VMEM((lp, cw), F32), pltpu.VMEM((lp, cw), F32),
                        pltpu.VMEM((D_STATE, cw), F32), pltpu.VMEM((D_STATE, cw), F32),
                        pltpu.VMEM((lp, LANES), F32),
                        pltpu.VMEM((2 * hs, lp), F32)],
        compiler_params=_cparams(2),
        name="ssd",
    )(xbc, xbc, xbc, z, w["conv_w"], w["conv_b"], w["conv_w"], w["conv_b"], w["conv_w"], w["conv_b"],
      dt_col, dt_row, w["ssd_pcol"], w["ssd_prow"])


def _attn_kernel(q_ref, k_ref, v_ref, bias_ref, o_ref, *, tq):
    i = pl.program_id(2)
    rows = ATTN_ROWS
    for part in range(tq // rows):
        r0 = pl.multiple_of(CHUNK + i * tq + part * rows, CHUNK)
        q = q_ref[0, 0, pl.ds(r0, rows), :]
        s = lax.dot_general(q, k_ref[0, 0], (((1,), (1,)), ((), ())), preferred_element_type=F32)
        s = s + bias_ref[...]
        m = jnp.max(s, axis=-1, keepdims=True)
        p = jnp.exp(s - m)
        l = jnp.sum(p, axis=-1, keepdims=True)
        o = jnp.dot(p.astype(BF16), v_ref[0, 0], preferred_element_type=F32)
        o_ref[0, part * rows:(part + 1) * rows, :] = o / l


def _attention(q, k, v, key_bias):
    b, nh, lp, _ = q.shape
    s = lp - CHUNK
    tq = next(t for t in (2048, 1024, 512, 256) if s % t == 0)
    kv_spec = pl.BlockSpec((1, 1, lp, HEAD_PAD), lambda i, h, j: (i, h, 0, 0))
    return pl.pallas_call(
        functools.partial(_attn_kernel, tq=tq),
        grid=(b, nh, s // tq),
        in_specs=[kv_spec, kv_spec, kv_spec, _full((1, lp))],
        out_specs=pl.BlockSpec((1, tq, HEAD_PAD), lambda i, h, j: (i, j, h)),
        out_shape=jax.ShapeDtypeStruct((b, s, D_HEADS_PAD), F32),
        compiler_params=_cparams(3),
        name="attention",
    )(q, k, v, key_bias)


def _outproj_kernel(x_ref, y_ref, o_ref, lng_ref, lnb_ref, sg_ref, ag_ref, w1_ref, w2_ref,
                    g1_ref, b1_ref, h1_ref):
    h0 = _ln(x_ref[0], lng_ref[...], lnb_ref[...])
    ysn = _rms(y_ref[0], sg_ref[...], D_SSM)
    on = _rms(o_ref[0], ag_ref[...], D_ATTN)
    mix = _mm(ysn, w1_ref[...]) + _mm(on, w2_ref[...])
    h1_ref[0] = _ln(DEEPNORM_ALPHA * h0 + mix, g1_ref[...], b1_ref[...])


def _outproj(x, yg, o, w):
    b, s, _ = x.shape
    tb = next(t for t in (512, 256, 128) if s % t == 0)
    blk = lambda width: pl.BlockSpec((1, tb, width), lambda i, j: (i, j, 0))
    vec = lambda width: _full((1, width))
    return pl.pallas_call(
        _outproj_kernel,
        grid=(b, s // tb),
        in_specs=[blk(D_MODEL), blk(D_SSM), blk(D_HEADS_PAD), vec(D_MODEL), vec(D_MODEL), vec(D_SSM),
                  vec(D_HEADS_PAD), _full((D_SSM, D_MODEL)), _full((D_HEADS_PAD, D_MODEL)),
                  vec(D_MODEL), vec(D_MODEL)],
        out_specs=blk(D_MODEL),
        out_shape=jax.ShapeDtypeStruct((b, s, D_MODEL), F32),
        compiler_params=_cparams(2),
        name="outproj",
    )(x, yg, o, w["ln_in_g"], w["ln_in_b"], w["ssm_norm_g"], w["attn_norm_g"], w["w_out_ssm"],
      w["w_out_attn"], w["ln1_g"], w["ln1_b"])


def _topk_rows(s, vals_s, idx_s, row0, payload=None):
    n = s.shape[0]
    iota = lax.broadcasted_iota(I32, s.shape, 0)
    for r in range(PEER_TOPK):
        m = jnp.max(s, axis=0, keepdims=True)
        win = jnp.min(jnp.where(s == m, iota, n), axis=0, keepdims=True)
        hit = iota == win
        vals_s[row0 + r:row0 + r + 1, :] = m
        if payload is None:
            idx_s[row0 + r:row0 + r + 1, :] = win
        else:
            idx_s[row0 + r:row0 + r + 1, :] = jnp.sum(jnp.where(hit, payload, 0), axis=0, keepdims=True)
        s = jnp.where(hit, -jnp.inf, s)


_CAND_ROWS = [PEER_TOPK // (i + 1) for i in range(PEER_TOPK)]
_N_CAND = sum(_CAND_ROWS)
_N_CAND_PAD = -(-_N_CAND // SUBLANES) * SUBLANES


def _route_kernel(h_ref, wq_ref, keys_ref, idx_ref, idxp_ref, g_ref,
                  v1_s, i1_s, v2_s, i2_s, cand_s, cidx_s, top_s, exp_s, gate_s, *, tb):
    q = _mm(h_ref[...], wq_ref[...]).astype(BF16)
    cand_s[...] = jnp.full((_N_CAND_PAD, tb), -jnp.inf, F32)
    cidx_s[...] = jnp.zeros((_N_CAND_PAD, tb), I32)
    for hd in range(PEER_HEADS):
        q1 = q[:, hd * D_KEY:hd * D_KEY + D_SUBKEY]
        q2 = q[:, hd * D_KEY + D_SUBKEY:(hd + 1) * D_KEY]
        s1 = lax.dot_general(keys_ref[0, hd], q1, (((1,), (1,)), ((), ())), preferred_element_type=F32)
        s2 = lax.dot_general(keys_ref[1, hd], q2, (((1,), (1,)), ((), ())), preferred_element_type=F32)
        _topk_rows(s1, v1_s, i1_s, 0)
        _topk_rows(s2, v2_s, i2_s, 0)
        off = 0
        for i, n in enumerate(_CAND_ROWS):
            cand_s[off:off + n, :] = v1_s[i:i + 1, :] + v2_s[0:n, :]
            cidx_s[off:off + n, :] = i1_s[i:i + 1, :] * N_KEYS + i2_s[0:n, :]
            off += n
        _topk_rows(cand_s[...], top_s, exp_s, hd * PEER_TOPK, payload=cidx_s[...])
        ts = top_s[hd * PEER_TOPK:(hd + 1) * PEER_TOPK, :]
        e = jnp.exp(ts - jnp.max(ts, axis=0, keepdims=True))
        gate_s[hd * PEER_TOPK:(hd + 1) * PEER_TOPK, :] = e / jnp.sum(e, axis=0, keepdims=True)
    rows = exp_s[...] * SLAB
    idx_ref[...] = rows.T
    idxp_ref[...] = rows
    g_ref[...] = gate_s[...].T


def _route(h1, w):
    t = h1.shape[0]
    tb = 256 if t % 256 == 0 else 128
    return pl.pallas_call(
        functools.partial(_route_kernel, tb=tb),
        grid=(t // tb,),
        in_specs=[pl.BlockSpec((tb, D_MODEL), lambda i: (i, 0)),
                  _full((D_MODEL, PEER_HEADS * D_KEY)),
                  _full((2, PEER_HEADS, N_KEYS, D_SUBKEY))],
        out_specs=[pl.BlockSpec((tb, PAIRS), lambda i: (i, 0)), pl.BlockSpec((PAIRS, tb), lambda i: (i, 0)),
                   pl.BlockSpec((tb, PAIRS), lambda i: (i, 0))],
        out_shape=[jax.ShapeDtypeStruct((t, PAIRS), I32), jax.ShapeDtypeStruct((t // tb * PAIRS, tb), I32),
                   jax.ShapeDtypeStruct((t, PAIRS), F32)],
        scratch_shapes=[pltpu.VMEM((PEER_TOPK, tb), F32), pltpu.VMEM((PEER_TOPK, tb), I32),
                        pltpu.VMEM((PEER_TOPK, tb), F32), pltpu.VMEM((PEER_TOPK, tb), I32),
                        pltpu.VMEM((_N_CAND_PAD, tb), F32), pltpu.VMEM((_N_CAND_PAD, tb), I32),
                        pltpu.VMEM((PAIRS, tb), F32), pltpu.VMEM((PAIRS, tb), I32),
                        pltpu.VMEM((PAIRS, tb), F32)],
        compiler_params=_cparams(1),
        name="peer_route",
    )(h1, w["peer_w_query"], w["peer_sub_keys"])


def _rows_f32(slab):
    return pltpu.bitcast(slab, BF16).astype(F32)


def _gelu(x):
    return 0.5 * x * (1.0 + lax.erf(x * (1.0 / math.sqrt(2.0))))


def _peer_u_kernel(idx_ref, x_ref, g_ref, tab_ref, w_ref, prod_a, prod_b, act_s, *, tb):
    ones = jnp.ones((SUBLANES, LANES), BF16)
    ri = lax.broadcasted_iota(I32, (PAIR_ROWS, PAIRS), 0)
    ci = lax.broadcasted_iota(I32, (PAIR_ROWS, PAIRS), 1)
    group = (lax.shift_right_logical(ri, 2) == ci).astype(BF16)
    sub = lax.broadcasted_iota(I32, (SUBLANES, PAIR_ROWS), 0)

    def gather(t0, dst):
        for n in range(U_PHASE):
            t = t0 + n
            xt = x_ref[pl.ds(pl.multiple_of(t * SUBLANES, SUBLANES), SUBLANES), :]
            base = t * PAIRS
            for k in range(PAIRS):
                row = pl.multiple_of(idx_ref[base + k], SLAB)
                p = _rows_f32(tab_ref[pl.ds(row, SLAB), :]) * xt
                dst[n * PAIR_ROWS + k * SLAB:n * PAIR_ROWS + (k + 1) * SLAB, :] = p[0:SLAB] + p[SLAB:2 * SLAB]

    def reduce(src, t0):
        sums = lax.dot_general(ones, src[...].astype(BF16), (((1,), (1,)), ((), ())),
                               preferred_element_type=F32)
        per_tok = sums[:, 0:PAIR_ROWS]
        for n in range(1, U_PHASE):
            per_tok = jnp.where(sub == n, sums[:, n * PAIR_ROWS:(n + 1) * PAIR_ROWS], per_tok)
        act_s[pl.ds(pl.multiple_of(t0, U_PHASE), U_PHASE), :] = jnp.dot(
            per_tok.astype(BF16), group, preferred_element_type=F32)

    prod_b[...] = jnp.zeros((U_PHASE * PAIR_ROWS, LANES), F32)

    def body(j, carry):
        t0 = 2 * U_PHASE * j
        gather(t0, prod_a)
        reduce(prod_b, jnp.maximum(t0 - U_PHASE, 0))
        gather(t0 + U_PHASE, prod_b)
        reduce(prod_a, t0)
        return carry

    lax.fori_loop(0, tb // (2 * U_PHASE), body, 0)
    reduce(prod_b, tb - U_PHASE)
    w_ref[...] = g_ref[...] * _gelu(act_s[...])


def _peer_v_kernel(idx_ref, wt_ref, tab_ref, out_ref, wt_s, wb_a, wb_b, *, tb):
    wt_s[...] = wt_ref[...].T
    lane = lax.broadcasted_iota(I32, (PAIRS, tb), 1)

    def spread(t0, dst):
        for n in range(V_PHASE):
            col = jnp.sum(jnp.where(lane == t0 + n, wt_s[...], 0.0), axis=1, keepdims=True)
            dst[n] = jnp.broadcast_to(col, (PAIRS, LANES))

    def phase(t0, wb):
        toks = [t0 + n for n in range(V_PHASE)]

        def step(kc, accs):
            accs = list(accs)
            for kk in range(V_STEP):
                k = kc * V_STEP + kk
                offsets = idx_ref.at[k]
                for n in range(V_PHASE):
                    row = pl.multiple_of(offsets[toks[n]], SLAB)
                    accs[n] = accs[n] + wb[n, pl.ds(k, 1), :] * _rows_f32(tab_ref[pl.ds(row, SLAB), :])
            return tuple(accs)

        zeros = tuple(jnp.zeros((SUBLANES, LANES), F32) for _ in range(V_PHASE))
        accs = lax.fori_loop(0, PAIRS // V_STEP, step, zeros)
        for n in range(V_PHASE):
            out_ref[pl.ds(pl.multiple_of(toks[n] * SUBLANES, SUBLANES), SUBLANES), :] = accs[n]

    spread(0, wb_a)

    def body(j, carry):
        t0 = 2 * V_PHASE * j
        spread(t0 + V_PHASE, wb_b)
        phase(t0, wb_a)
        spread(jnp.minimum(t0 + 2 * V_PHASE, tb - V_PHASE), wb_a)
        phase(t0 + V_PHASE, wb_b)
        return carry

    lax.fori_loop(0, tb // (2 * V_PHASE), body, 0)


def _peer_tb(t):
    return next(c for c in (128, 64, 32, 16) if t % c == 0)


def _table_spec(n_rows):
    return pl.BlockSpec((n_rows, LANES), lambda i: (0, 0), pipeline_mode=pl.Buffered(1))


def _peer_u(idx_flat, x8, gates, table):
    t = gates.shape[0]
    tb = _peer_tb(t)
    return pl.pallas_call(
        functools.partial(_peer_u_kernel, tb=tb),
        grid=(t // tb,),
        in_specs=[pl.BlockSpec((tb * PAIRS,), lambda i: (i,), memory_space=pltpu.SMEM),
                  pl.BlockSpec((tb * SUBLANES, LANES), lambda i: (i, 0)),
                  pl.BlockSpec((tb, PAIRS), lambda i: (i, 0)),
                  _table_spec(table.shape[0])],
        out_specs=pl.BlockSpec((tb, PAIRS), lambda i: (i, 0)),
        out_shape=jax.ShapeDtypeStruct((t, PAIRS), F32),
        scratch_shapes=[pltpu.VMEM((U_PHASE * PAIR_ROWS, LANES), F32),
                        pltpu.VMEM((U_PHASE * PAIR_ROWS, LANES), F32), pltpu.VMEM((tb, PAIRS), F32)],
        compiler_params=_cparams(1),
        name="peer_u",
    )(idx_flat, x8, gates, table)


def _peer_v(idx_pairs, wt, table):
    t = wt.shape[0]
    tb = _peer_tb(t)
    per_route = idx_pairs.shape[1] // tb
    return pl.pallas_call(
        functools.partial(_peer_v_kernel, tb=tb),
        grid=(t // tb,),
        in_specs=[pl.BlockSpec((PAIRS, tb), lambda i: (i // per_route, i % per_route), memory_space=pltpu.SMEM),
                  pl.BlockSpec((tb, PAIRS), lambda i: (i, 0)),
                  _table_spec(table.shape[0])],
        out_specs=pl.BlockSpec((tb * SUBLANES, LANES), lambda i: (i, 0)),
        out_shape=jax.ShapeDtypeStruct((t * SUBLANES, LANES), F32),
        scratch_shapes=[pltpu.VMEM((PAIRS, tb), F32), pltpu.VMEM((V_PHASE, PAIRS, LANES), F32),
                        pltpu.VMEM((V_PHASE, PAIRS, LANES), F32)],
        compiler_params=_cparams(1),
        name="peer_v",
    )(idx_pairs, wt, table)


def _ln2_kernel(h_ref, p_ref, g_ref, b_ref, o_ref):
    o_ref[...] = _ln(DEEPNORM_ALPHA * h_ref[...] + p_ref[...], g_ref[...], b_ref[...])


def _ln2(h1, peer, w):
    t = h1.shape[0]
    tb = next(c for c in (512, 256, 128) if t % c == 0)
    blk = pl.BlockSpec((tb, D_MODEL), lambda i: (i, 0))
    return pl.pallas_call(
        _ln2_kernel,
        grid=(t // tb,),
        in_specs=[blk, blk, _full((1, D_MODEL)), _full((1, D_MODEL))],
        out_specs=blk,
        out_shape=jax.ShapeDtypeStruct((t, D_MODEL), F32),
        compiler_params=_cparams(1),
        name="ln2",
    )(h1, peer, w["ln2_g"], w["ln2_b"])


def _pack_table(tab):
    e = tab.shape[0]
    bits = lax.bitcast_convert_type(tab.astype(BF16), jnp.uint16).astype(U32).reshape(e, SLAB, 2, LANES)
    return (bits[:, :, 0] | (bits[:, :, 1] << 16)).reshape(e * SLAB, LANES)


def _rot_cols(wr):
    half = QK_ROPE_DIM // 2
    return jnp.concatenate([-wr[..., half:], wr[..., :half]], axis=-1)


def _prep_weights(ln_in_g, ln_in_b, w_in, conv_w, conv_b, dt_bias_fwd, dt_bias_bwd, a_log_fwd, a_log_bwd,
                  d_skip, ssm_norm_g, q_norm_g, w_uq, kv_norm_g, w_ukv, attn_norm_g, w_out, ln1_g, ln1_b,
                  peer_w_query, peer_sub_keys, peer_u, peer_v, ln2_g, ln2_b):
    li = 0
    row = lambda v: v.reshape(1, -1).astype(F32)
    w = {}
    w["ln_in_g"], w["ln_in_b"] = row(ln_in_g), row(ln_in_b)
    wi = w_in[li]
    w_kr = wi[:, SPLIT_CKV:]
    zc = lambda n: jnp.zeros((D_MODEL, n), F32)
    tail_a = jnp.concatenate([wi[:, SPLIT_XBC:SPLIT_DT], w_kr, zc(LANES - KR_LANE - QK_ROPE_DIM)], axis=1)
    tail_b = jnp.concatenate([zc(KR_LANE), _rot_cols(w_kr), zc(LANES - KR_LANE - QK_ROPE_DIM)], axis=1)
    w["w_proj"] = jnp.concatenate([wi[:, :SPLIT_XBC], wi[:, SPLIT_DT:SPLIT_CKV], tail_a, tail_b],
                                  axis=1).astype(BF16)
    w["conv_w"], w["conv_b"] = conv_w[li].astype(F32), row(conv_b[li])
    hs = SSD_HEADS_PER_STEP
    ng = SSM_HEADS // hs
    per_dir = lambda f, b_: jnp.concatenate([f[li].reshape(ng, hs), b_[li].reshape(ng, hs)], axis=1)
    dtb, alog = per_dir(dt_bias_fwd, dt_bias_bwd), per_dir(a_log_fwd, a_log_bwd)
    dsk = per_dir(d_skip, d_skip)
    pcol = jnp.stack([dtb, alog, dsk], axis=1)
    w["ssd_pcol"] = jnp.pad(pcol, ((0, 0), (0, 0), (0, LANES - 2 * hs))).astype(F32)
    w["ssd_prow"] = jnp.stack([dtb, alog], axis=2).astype(F32)
    w["ssm_norm_g"] = row(ssm_norm_g[li])
    w["q_norm_g"], w["kv_norm_g"] = row(q_norm_g[li]), row(kv_norm_g[li])
    pad_h = lambda a, n: jnp.pad(a, ((0, 0), (0, 0), (0, HEAD_PAD - n)))
    uq = w_uq[li].reshape(Q_LORA_RANK, MLA_HEADS, QK_NOPE_DIM + QK_ROPE_DIM)
    uq_rot = jnp.concatenate([jnp.zeros_like(uq[..., :QK_NOPE_DIM]), _rot_cols(uq[..., QK_NOPE_DIM:])], axis=-1)
    w["w_uq"] = pad_h(uq, QK_NOPE_DIM + QK_ROPE_DIM).reshape(Q_LORA_RANK, D_HEADS_PAD).astype(BF16)
    w["w_uq_rot"] = pad_h(uq_rot, QK_NOPE_DIM + QK_ROPE_DIM).reshape(Q_LORA_RANK, D_HEADS_PAD).astype(BF16)
    ukv = w_ukv[li].reshape(KV_LORA_RANK, MLA_HEADS, QK_NOPE_DIM + V_HEAD_DIM)
    w["w_uk"] = pad_h(ukv[..., :QK_NOPE_DIM], QK_NOPE_DIM).reshape(KV_LORA_RANK, D_HEADS_PAD).astype(BF16)
    w["w_uv"] = pad_h(ukv[..., QK_NOPE_DIM:], V_HEAD_DIM).reshape(KV_LORA_RANK, D_HEADS_PAD).astype(BF16)
    src = KR_LANE + jnp.arange(QK_ROPE_DIM)
    e_kr = jnp.zeros((LANES, MLA_HEADS, HEAD_PAD), F32)
    e_kr = e_kr.at[src, :, QK_NOPE_DIM + jnp.arange(QK_ROPE_DIM)].set(1.0)
    w["e_kr"] = e_kr.reshape(LANES, D_HEADS_PAD).astype(BF16)
    w["attn_norm_g"] = pad_h(attn_norm_g[li].reshape(1, MLA_HEADS, V_HEAD_DIM), V_HEAD_DIM).reshape(1, D_HEADS_PAD)
    wo = w_out[li]
    w["w_out_ssm"] = wo[:D_SSM].astype(BF16)
    wo_attn = wo[D_SSM:].reshape(MLA_HEADS, V_HEAD_DIM, D_MODEL)
    w["w_out_attn"] = jnp.pad(wo_attn, ((0, 0), (0, HEAD_PAD - V_HEAD_DIM), (0, 0))).reshape(
        D_HEADS_PAD, D_MODEL).astype(BF16)
    w["ln1_g"], w["ln1_b"] = row(ln1_g[li]), row(ln1_b[li])
    w["peer_w_query"] = peer_w_query[li].astype(BF16)
    w["peer_sub_keys"] = peer_sub_keys[li].astype(BF16)
    w["u_table"] = _pack_table(peer_u[li])
    w["v_table"] = _pack_table(peer_v[li])
    w["ln2_g"], w["ln2_b"] = row(ln2_g[li]), row(ln2_b[li])
    return w


def _rope_tables(lp):
    half = QK_ROPE_DIM // 2
    pos = jnp.arange(lp, dtype=F32) - float(META_PAD)
    inv = ROPE_THETA ** (-jnp.arange(half, dtype=F32) / half)
    ang = pos[:, None] * inv[None, :]
    cos2 = jnp.tile(jnp.cos(ang), (1, 2))
    sin2 = jnp.tile(jnp.sin(ang), (1, 2))
    scale = 1.0 / math.sqrt(QK_NOPE_DIM + QK_ROPE_DIM)
    tail = HEAD_PAD - QK_NOPE_DIM - QK_ROPE_DIM
    cq = jnp.concatenate([jnp.ones((lp, QK_NOPE_DIM), F32), cos2, jnp.zeros((lp, tail), F32)], axis=1)
    sq = jnp.concatenate([jnp.zeros((lp, QK_NOPE_DIM), F32), sin2, jnp.zeros((lp, tail), F32)], axis=1)
    place = lambda a: jnp.pad(a, ((0, 0), (KR_LANE, LANES - KR_LANE - QK_ROPE_DIM)))
    key_bias = jnp.where(jnp.arange(lp) >= META_PAD, 0.0, -jnp.inf).astype(F32).reshape(1, lp)
    return {"cos_q": jnp.tile(cq, (1, MLA_HEADS)) * scale, "sin_q": jnp.tile(sq, (1, MLA_HEADS)) * scale,
            "cos_k": place(cos2), "sin_k": place(sin2), "key_bias": key_bias}


def _encode(x, meta_tokens, w):
    b, s, _ = x.shape
    lp = s + CHUNK
    lead = jnp.concatenate([jnp.zeros((META_PAD, D_MODEL), F32), meta_tokens.astype(F32)], axis=0)
    xpad = jnp.concatenate([jnp.broadcast_to(lead[None], (b, CHUNK, D_MODEL)), x], axis=1)
    tabs = _rope_tables(lp)
    z, xbc, tail, q, k, v = _inproj(xpad, tabs, w)
    hs = SSD_HEADS_PER_STEP
    ng = SSM_HEADS // hs
    dt = tail[..., :2 * SSM_HEADS].reshape(b, lp, 2, ng, hs)
    dt = jnp.transpose(dt, (0, 3, 1, 2, 4)).reshape(b, ng, lp, 2 * hs)
    dt_col = jnp.pad(dt, ((0, 0), (0, 0), (0, 0), (0, LANES - 2 * hs)))
    dt_row = jnp.swapaxes(dt, 2, 3)
    yg = _ssd(z, xbc, dt_col, dt_row, w)
    o = _attention(q, k, v, tabs["key_bias"])
    h1 = _outproj(x, yg, o, w).reshape(b * s, D_MODEL)
    idx, idx_pairs, gates = _route(h1, w)
    wt = _peer_u(idx.reshape(-1), h1.reshape(b * s * SUBLANES, LANES), gates, w["u_table"])
    peer = _peer_v(idx_pairs, wt, w["v_table"])
    return _ln2(h1, peer.reshape(b * s, D_MODEL), w).reshape(b, s, D_MODEL)


def kernel(x_prompt, x_sample, meta_tokens, ln_in_g, ln_in_b, w_in, conv_w, conv_b, dt_bias_fwd, dt_bias_bwd,
           a_log_fwd, a_log_bwd, d_skip, ssm_norm_g, q_norm_g, w_uq, kv_norm_g, w_ukv, attn_norm_g, w_out,
           ln1_g, ln1_b, peer_w_query, peer_sub_keys, peer_u, peer_v, ln2_g, ln2_b):
    w = _prep_weights(ln_in_g, ln_in_b, w_in, conv_w, conv_b, dt_bias_fwd, dt_bias_bwd, a_log_fwd, a_log_bwd,
                      d_skip, ssm_norm_g, q_norm_g, w_uq, kv_norm_g, w_ukv, attn_norm_g, w_out, ln1_g, ln1_b,
                      peer_w_query, peer_sub_keys, peer_u, peer_v, ln2_g, ln2_b)
    return (_encode(x_prompt, meta_tokens, w), _encode(x_sample, meta_tokens, w))
```

```python
import functools
import math

import jax
import jax.numpy as jnp
from jax import lax
from jax.experimental import pallas as pl
from jax.experimental.pallas import tpu as pltpu

F32 = jnp.float32
BF16 = jnp.bfloat16
I32 = jnp.int32
U32 = jnp.uint32

D_MODEL = 1024
N_META = 16
SSM_HEADS = 16
SSM_HEAD_DIM = 64
D_SSM = SSM_HEADS * SSM_HEAD_DIM
SSM_GROUPS = 2
D_STATE = 128
D_CONV = 5
D_CONV_CH = D_SSM + 2 * SSM_GROUPS * D_STATE
CHUNK = 128
META_PAD = CHUNK - N_META
MLA_HEADS = 8
QK_NOPE_DIM = 64
QK_ROPE_DIM = 32
V_HEAD_DIM = 64
Q_LORA_RANK = 384
KV_LORA_RANK = 256
D_ATTN = MLA_HEADS * V_HEAD_DIM
ROPE_THETA = 10000.0
SPLIT_Z = D_SSM
SPLIT_XBC = SPLIT_Z + D_CONV_CH
SPLIT_DT = SPLIT_XBC + 2 * SSM_HEADS
SPLIT_CQ = SPLIT_DT + Q_LORA_RANK
SPLIT_CKV = SPLIT_CQ + KV_LORA_RANK
PEER_HEADS = 8
N_KEYS = 128
PEER_TOPK = 16
D_KEY = 256
D_SUBKEY = D_KEY // 2
DEPTH = 1
DEEPNORM_ALPHA = (2.0 * DEPTH) ** 0.25
EPS = 1e-5

LANES = 128
SUBLANES = 8
HEAD_PAD = LANES
D_HEADS_PAD = MLA_HEADS * HEAD_PAD
VMEM_LIMIT = 56 * 1024 * 1024
SSD_HEADS_PER_STEP = 8
SSD_UNROLL = 2
ATTN_ROWS = 256
PAIRS = PEER_HEADS * PEER_TOPK
SLAB = 4
PAIR_ROWS = PAIRS * SLAB
U_PHASE = 2 * SUBLANES
V_PHASE = 4
V_STEP = 32
V_SPREAD = 2 * V_PHASE

OFF_Z = 0
OFF_XBC = D_SSM
OFF_CQ = OFF_XBC + D_CONV_CH
OFF_CKV = OFF_CQ + Q_LORA_RANK
OFF_TAIL_A = OFF_CKV + KV_LORA_RANK
OFF_TAIL_B = OFF_TAIL_A + LANES
D_PROJ = OFF_TAIL_B + LANES
KR_LANE = 2 * SSM_HEADS


def _cparams(n_axes):
    return pltpu.CompilerParams(dimension_semantics=("arbitrary",) * n_axes,
                                vmem_limit_bytes=VMEM_LIMIT)


def _full(shape):
    zeros = (0,) * len(shape)
    return pl.BlockSpec(shape, lambda *_: zeros)


def _ln(x, g, b):
    mu = jnp.mean(x, axis=-1, keepdims=True)
    xc = x - mu
    var = jnp.mean(xc * xc, axis=-1, keepdims=True)
    return xc * lax.rsqrt(var + EPS) * g + b


def _rms(x, g, n):
    ms = jnp.sum(x * x, axis=-1, keepdims=True) * (1.0 / n)
    return x * lax.rsqrt(ms + EPS) * g


def _mm(a, b):
    return jnp.dot(a.astype(BF16), b.astype(BF16), preferred_element_type=F32)


def _mm_nt(a, b):
    return lax.dot_general(a.astype(BF16), b.astype(BF16), (((1,), (1,)), ((), ())),
                           preferred_element_type=F32)


def _mm_exact(a, b):
    return jnp.dot(a, b, preferred_element_type=F32, precision=lax.Precision.HIGHEST)


def _sigmoid(x):
    return 1.0 / (1.0 + jnp.exp(-x))


def _softplus(x):
    return jnp.maximum(x, 0.0) + jnp.log(1.0 + jnp.exp(-jnp.abs(x)))


def _inproj_kernel(x_ref, cq_ref, sq_ref, ck_ref, sk_ref, lng_ref, lnb_ref, win_ref, qng_ref,
                   wuq_ref, wuqr_ref, kvng_ref, wuk_ref, wuv_ref, ekr_ref,
                   z_ref, xbc_ref, tail_ref, q_ref, k_ref, v_ref, *, tm):
    j = pl.program_id(1)
    h = _ln(x_ref[0], lng_ref[...], lnb_ref[...])
    proj = _mm(h, win_ref[...])
    row = j * tm + lax.broadcasted_iota(I32, (tm, 1), 0)
    valid = row >= META_PAD
    z_ref[0] = proj[:, OFF_Z:OFF_XBC]
    xbc_ref[0] = jnp.where(valid, proj[:, OFF_XBC:OFF_CQ], 0.0)
    tail = proj[:, OFF_TAIL_A:OFF_TAIL_B]
    tail_ref[0] = tail
    cqn = _rms(proj[:, OFF_CQ:OFF_CKV], qng_ref[...], Q_LORA_RANK)
    qf = _mm(cqn, wuq_ref[...]) * cq_ref[...] + _mm(cqn, wuqr_ref[...]) * sq_ref[...]
    ckvn = _rms(proj[:, OFF_CKV:OFF_TAIL_A], kvng_ref[...], KV_LORA_RANK)
    kt = tail * ck_ref[...] + proj[:, OFF_TAIL_B:D_PROJ] * sk_ref[...]
    kf = _mm(ckvn, wuk_ref[...]) + _mm(kt, ekr_ref[...])
    vf = _mm(ckvn, wuv_ref[...])
    for hd in range(MLA_HEADS):
        sl = slice(hd * HEAD_PAD, (hd + 1) * HEAD_PAD)
        q_ref[0, hd] = qf[:, sl].astype(BF16)
        k_ref[0, hd] = kf[:, sl].astype(BF16)
        v_ref[0, hd] = vf[:, sl].astype(BF16)


def _inproj(xpad, tabs, w):
    b, lp, _ = xpad.shape
    tm = next(t for t in (544, 384, 256, 128) if lp % t == 0)
    grid = (b, lp // tm)
    row_spec = lambda width: pl.BlockSpec((tm, width), lambda i, j: (j, 0))
    seq_spec = lambda width: pl.BlockSpec((1, tm, width), lambda i, j: (i, j, 0))
    head_spec = pl.BlockSpec((1, MLA_HEADS, tm, HEAD_PAD), lambda i, j: (i, 0, j, 0))
    head_shape = jax.ShapeDtypeStruct((b, MLA_HEADS, lp, HEAD_PAD), BF16)
    return pl.pallas_call(
        functools.partial(_inproj_kernel, tm=tm),
        grid=grid,
        in_specs=[seq_spec(D_MODEL), row_spec(D_HEADS_PAD), row_spec(D_HEADS_PAD), row_spec(LANES),
                  row_spec(LANES), _full((1, D_MODEL)), _full((1, D_MODEL)), _full((D_MODEL, D_PROJ)),
                  _full((1, Q_LORA_RANK)), _full((Q_LORA_RANK, D_HEADS_PAD)),
                  _full((Q_LORA_RANK, D_HEADS_PAD)), _full((1, KV_LORA_RANK)),
                  _full((KV_LORA_RANK, D_HEADS_PAD)), _full((KV_LORA_RANK, D_HEADS_PAD)),
                  _full((LANES, D_HEADS_PAD))],
        out_specs=[seq_spec(D_SSM), seq_spec(D_CONV_CH), seq_spec(LANES), head_spec, head_spec, head_spec],
        out_shape=[jax.ShapeDtypeStruct((b, lp, D_SSM), F32),
                   jax.ShapeDtypeStruct((b, lp, D_CONV_CH), F32),
                   jax.ShapeDtypeStruct((b, lp, LANES), F32),
                   head_shape, head_shape, head_shape],
        compiler_params=_cparams(2),
        name="inproj",
    )(xpad, tabs["cos_q"], tabs["sin_q"], tabs["cos_k"], tabs["sin_k"], w["ln_in_g"], w["ln_in_b"],
      w["w_proj"], w["q_norm_g"], w["w_uq"], w["w_uq_rot"], w["kv_norm_g"], w["w_uk"], w["w_uv"],
      w["e_kr"])


def _expand_heads(v, lane0, n_heads):
    lane = lax.broadcasted_iota(I32, (v.shape[0], LANES), 1)
    tiles = []
    for t in range(n_heads // 2):
        a = v[:, lane0 + 2 * t:lane0 + 2 * t + 1]
        b = v[:, lane0 + 2 * t + 1:lane0 + 2 * t + 2]
        tiles.append(jnp.where(lane < SSM_HEAD_DIM, a, b))
    return jnp.concatenate(tiles, axis=1)


def _conv_silu_chunk(win, w, bias, o, n):
    acc = bias + w[0:1] * win[o - 2:o - 2 + n]
    for k in range(1, D_CONV):
        acc = acc + w[k:k + 1] * win[o + k - 2:o + k - 2 + n]
    return acc * _sigmoid(acc)


def _conv_silu(src_ref, w_ref, b_ref, dst, nc):
    w = w_ref[...]
    bias = b_ref[...]
    lp = nc * CHUNK
    halo = SUBLANES
    width = src_ref.shape[-1]
    win = src_ref[0, 0:CHUNK + 2 * halo, :]
    dst[0:halo, :] = jnp.zeros((halo, width), F32)
    dst[halo:CHUNK, :] = _conv_silu_chunk(win, w, bias, halo, CHUNK - halo)

    def mid(c, carry):
        start = pl.multiple_of(c * CHUNK - halo, halo)
        win = src_ref[0, pl.ds(start, CHUNK + 2 * halo), :]
        dst[pl.ds(pl.multiple_of(c * CHUNK, CHUNK), CHUNK), :] = _conv_silu_chunk(win, w, bias, halo, CHUNK)
        return carry

    lax.fori_loop(1, nc - 1, mid, 0)
    win = jnp.concatenate([src_ref[0, lp - CHUNK - 2 * halo:lp, :], jnp.zeros((halo, width), F32)], axis=0)
    dst[lp - CHUNK:lp, :] = _conv_silu_chunk(win, w, bias, 2 * halo, CHUNK)


def _ssd_kernel(xs_ref, bm_ref, cm_ref, z_ref, cwx_ref, cbx_ref, cwb_ref, cbb_ref, cwc_ref, cbc_ref,
                dtc_ref, dtr_ref, pcol_ref, prow_ref, out_ref,
                xs_s, b_s, c_s, yf_s, yb_s, stf_s, stb_s, dtc_s, dtr_s, *, hs, nc):
    lp = nc * CHUNK
    cw = hs * SSM_HEAD_DIM
    pc = pcol_ref[0]
    pr = prow_ref[0]

    _conv_silu(xs_ref, cwx_ref, cbx_ref, xs_s, nc)
    _conv_silu(bm_ref, cwb_ref, cbb_ref, b_s, nc)
    _conv_silu(cm_ref, cwc_ref, cbc_ref, c_s, nc)

    rowi = lax.broadcasted_iota(I32, (lp, 1), 0)
    dtc_s[...] = jnp.where(rowi >= META_PAD, _softplus(dtc_ref[0, 0] + pc[0:1]), 0.0)
    lanei = lax.broadcasted_iota(I32, (1, lp), 1)
    dtr_s[...] = jnp.where(lanei >= META_PAD, _softplus(dtr_ref[0, 0] + pr[:, 0:1]), 0.0)
    a_col = -jnp.exp(pc[1:2])
    a_exp_f = _expand_heads(a_col, 0, hs)
    a_exp_b = _expand_heads(a_col, hs, hs)
    a_row = -jnp.exp(pr[:, 1:2])

    ri = lax.broadcasted_iota(I32, (CHUNK, CHUNK), 0)
    ci = lax.broadcasted_iota(I32, (CHUNK, CHUNK), 1)

    lane = lax.broadcasted_iota(I32, (CHUNK, LANES), 1)
    lower = ri >= ci
    upper = ri <= ci
    lower_f = lower.astype(F32)
    upper_f = upper.astype(F32)
    stf_s[...] = jnp.zeros((D_STATE, cw), F32)
    stb_s[...] = jnp.zeros((D_STATE, cw), F32)

    def scan_chunk(c, fwd, st_ref, y_ref):
        d0 = 0 if fwd else hs
        tri = lower if fwd else upper
        tmat = lower_f if fwd else upper_f
        umat = upper_f if fwd else lower_f
        r0 = pl.multiple_of(c * CHUNK, CHUNK)
        dte = _expand_heads(dtc_s[pl.ds(r0, CHUNK), :], d0, hs)
        cs_e = _mm_exact(tmat, dte * (a_exp_f if fwd else a_exp_b))
        a_r = dtr_s[:, pl.ds(r0, CHUNK)] * a_row
        cs_r = _mm_exact(a_r, umat)
        tot = cs_e[CHUNK - 1:CHUNK] if fwd else cs_e[0:1]
        x = xs_s[pl.ds(r0, CHUNK), :]
        bc = b_s[pl.ds(r0, CHUNK), :]
        cc = c_s[pl.ds(r0, CHUNK), :]
        xdt = x * dte
        cb = _mm_nt(cc, bc)
        st = st_ref[...]
        y = _mm(cc, st) * jnp.exp(cs_e)
        tiles = []
        for t in range(hs // 2):
            xt = xdt[:, t * LANES:(t + 1) * LANES]
            ys = []
            for hh in (2 * t, 2 * t + 1):
                seg = cs_e[:, hh * SSM_HEAD_DIM:hh * SSM_HEAD_DIM + 1] - cs_r[d0 + hh:d0 + hh + 1, :]
                lmat = jnp.exp(jnp.where(tri, seg, -jnp.inf))
                ys.append(_mm(cb * lmat, xt))
            tiles.append(jnp.where(lane < SSM_HEAD_DIM, ys[0], ys[1]))
        y_ref[pl.ds(r0, CHUNK), :] = y + jnp.concatenate(tiles, axis=1)
        xw = xdt * jnp.exp(tot - cs_e)
        st_ref[...] = st * jnp.exp(tot) + _mm(bc.T, xw)

    def step(i):
        scan_chunk(i, True, stf_s, yf_s)
        scan_chunk(nc - 1 - i, False, stb_s, yb_s)

    def body(j, carry):
        for u in range(SSD_UNROLL):
            step(j * SSD_UNROLL + u)
        return carry

    lax.fori_loop(0, nc // SSD_UNROLL, body, 0)
    for i in range(nc - nc % SSD_UNROLL, nc):
        step(i)

    d_skip = _expand_heads(pc[2:3], 0, hs)

    def fin(c, carry):
        r0 = pl.multiple_of(c * CHUNK, CHUNK)
        y = (yf_s[pl.ds(r0, CHUNK), :] + yb_s[pl.ds(r0, CHUNK), :]) + d_skip * xs_s[pl.ds(r0, CHUNK), :]
        zc = z_ref[0, pl.ds(r0, CHUNK), :]
        out_ref[0, pl.ds(pl.multiple_of(r0 - CHUNK, CHUNK), CHUNK), :] = y * (zc * _sigmoid(zc))
        return carry

    lax.fori_loop(1, nc, fin, 0)


def _ssd(z, xbc, dt_col, dt_row, w):
    b, lp, _ = z.shape
    nc = lp // CHUNK
    s = lp - CHUNK
    hs = SSD_HEADS_PER_STEP
    ng = SSM_HEADS // hs
    cw = hs * SSM_HEAD_DIM
    per_group = ng // SSM_GROUPS
    b_blk = D_SSM // D_STATE
    c_blk = b_blk + SSM_GROUPS
    x_map = lambda i, g: (i, 0, g)
    b_map = lambda i, g: (i, 0, b_blk + g // per_group)
    c_map = lambda i, g: (i, 0, c_blk + g // per_group)
    wx_map = lambda i, g: (0, g)
    wb_map = lambda i, g: (0, b_blk + g // per_group)
    wc_map = lambda i, g: (0, c_blk + g // per_group)
    return pl.pallas_call(
        functools.partial(_ssd_kernel, hs=hs, nc=nc),
        grid=(b, ng),
        in_specs=[pl.BlockSpec((1, lp, cw), x_map), pl.BlockSpec((1, lp, D_STATE), b_map),
                  pl.BlockSpec((1, lp, D_STATE), c_map), pl.BlockSpec((1, lp, cw), x_map),
                  pl.BlockSpec((D_CONV, cw), wx_map), pl.BlockSpec((1, cw), wx_map),
                  pl.BlockSpec((D_CONV, D_STATE), wb_map), pl.BlockSpec((1, D_STATE), wb_map),
                  pl.BlockSpec((D_CONV, D_STATE), wc_map), pl.BlockSpec((1, D_STATE), wc_map),
                  pl.BlockSpec((1, 1, lp, LANES), lambda i, g: (i, g, 0, 0)),
                  pl.BlockSpec((1, 1, 2 * hs, lp), lambda i, g: (i, g, 0, 0)),
                  pl.BlockSpec((1, 3, LANES), lambda i, g: (g, 0, 0)),
                  pl.BlockSpec((1, 2 * hs, 2), lambda i, g: (g, 0, 0))],
        out_specs=pl.BlockSpec((1, s, cw), x_map),
        out_shape=jax.ShapeDtypeStruct((b, s, D_SSM), F32),
        scratch_shapes=[pltpu.VMEM((lp, cw), F32), pltpu.VMEM((lp, D_STATE), F32),
                        pltpu.VMEM((lp, D_STATE), F32), pltpu.VMEM((lp, cw), F32), pltpu.VMEM((lp, cw), F32),
                        pltpu.VMEM((D_STATE, cw), F32), pltpu.VMEM((D_STATE, cw), F32),
                        pltpu.VMEM((lp, LANES), F32),
                        pltpu.VMEM((2 * hs, lp), F32)],
        compiler_params=_cparams(2),
        name="ssd",
    )(xbc, xbc, xbc, z, w["conv_w"], w["conv_b"], w["conv_w"], w["conv_b"], w["conv_w"], w["conv_b"],
      dt_col, dt_row, w["ssd_pcol"], w["ssd_prow"])


def _attn_kernel(q_ref, k_ref, v_ref, bias_ref, o_ref, *, tq):
    i = pl.program_id(2)
    rows = ATTN_ROWS
    for part in range(tq // rows):
        r0 = pl.multiple_of(CHUNK + i * tq + part * rows, CHUNK)
        q = q_ref[0, 0, pl.ds(r0, rows), :]
        s = lax.dot_general(q, k_ref[0, 0], (((1,), (1,)), ((), ())), preferred_element_type=F32)
        s = s + bias_ref[...]
        m = jnp.max(s, axis=-1, keepdims=True)
        p = jnp.exp(s - m)
        l = jnp.sum(p, axis=-1, keepdims=True)
        o = jnp.dot(p.astype(BF16), v_ref[0, 0], preferred_element_type=F32)
        o_ref[0, part * rows:(part + 1) * rows, :] = o / l


def _attention(q, k, v, key_bias):
    b, nh, lp, _ = q.shape
    s = lp - CHUNK
    tq = next(t for t in (2048, 1024, 512, 256) if s % t == 0)
    kv_spec = pl.BlockSpec((1, 1, lp, HEAD_PAD), lambda i, h, j: (i, h, 0, 0))
    return pl.pallas_call(
        functools.partial(_attn_kernel, tq=tq),
        grid=(b, nh, s // tq),
        in_specs=[kv_spec, kv_spec, kv_spec, _full((1, lp))],
        out_specs=pl.BlockSpec((1, tq, HEAD_PAD), lambda i, h, j: (i, j, h)),
        out_shape=jax.ShapeDtypeStruct((b, s, D_HEADS_PAD), F32),
        compiler_params=_cparams(3),
        name="attention",
    )(q, k, v, key_bias)


def _outproj_kernel(x_ref, y_ref, o_ref, lng_ref, lnb_ref, sg_ref, ag_ref, w1_ref, w2_ref,
                    g1_ref, b1_ref, h1_ref):
    h0 = _ln(x_ref[0], lng_ref[...], lnb_ref[...])
    ysn = _rms(y_ref[0], sg_ref[...], D_SSM)
    on = _rms(o_ref[0], ag_ref[...], D_ATTN)
    mix = _mm(ysn, w1_ref[...]) + _mm(on, w2_ref[...])
    h1_ref[0] = _ln(DEEPNORM_ALPHA * h0 + mix, g1_ref[...], b1_ref[...])


def _outproj(x, yg, o, w):
    b, s, _ = x.shape
    tb = next(t for t in (512, 256, 128) if s % t == 0)
    blk = lambda width: pl.BlockSpec((1, tb, width), lambda i, j: (i, j, 0))
    vec = lambda width: _full((1, width))
    return pl.pallas_call(
        _outproj_kernel,
        grid=(b, s // tb),
        in_specs=[blk(D_MODEL), blk(D_SSM), blk(D_HEADS_PAD), vec(D_MODEL), vec(D_MODEL), vec(D_SSM),
                  vec(D_HEADS_PAD), _full((D_SSM, D_MODEL)), _full((D_HEADS_PAD, D_MODEL)),
                  vec(D_MODEL), vec(D_MODEL)],
        out_specs=blk(D_MODEL),
        out_shape=jax.ShapeDtypeStruct((b, s, D_MODEL), F32),
        compiler_params=_cparams(2),
        name="outproj",
    )(x, yg, o, w["ln_in_g"], w["ln_in_b"], w["ssm_norm_g"], w["attn_norm_g"], w["w_out_ssm"],
      w["w_out_attn"], w["ln1_g"], w["ln1_b"])


def _topk_rows(s, vals_s, idx_s, row0, payload=None):
    n = s.shape[0]
    iota = lax.broadcasted_iota(I32, s.shape, 0)
    for r in range(PEER_TOPK):
        m = jnp.max(s, axis=0, keepdims=True)
        win = jnp.min(jnp.where(s == m, iota, n), axis=0, keepdims=True)
        hit = iota == win
        vals_s[row0 + r:row0 + r + 1, :] = m
        if payload is None:
            idx_s[row0 + r:row0 + r + 1, :] = win
        else:
            idx_s[row0 + r:row0 + r + 1, :] = jnp.sum(jnp.where(hit, payload, 0), axis=0, keepdims=True)
        s = jnp.where(hit, -jnp.inf, s)


_CAND_ROWS = [PEER_TOPK // (i + 1) for i in range(PEER_TOPK)]
_N_CAND = sum(_CAND_ROWS)
_N_CAND_PAD = -(-_N_CAND // SUBLANES) * SUBLANES


def _route_kernel(h_ref, wq_ref, keys_ref, idx_ref, idxp_ref, g_ref,
                  v1_s, i1_s, v2_s, i2_s, cand_s, cidx_s, top_s, exp_s, gate_s, *, tb):
    q = _mm(h_ref[...], wq_ref[...]).astype(BF16)
    cand_s[...] = jnp.full((_N_CAND_PAD, tb), -jnp.inf, F32)
    cidx_s[...] = jnp.zeros((_N_CAND_PAD, tb), I32)
    for hd in range(PEER_HEADS):
        q1 = q[:, hd * D_KEY:hd * D_KEY + D_SUBKEY]
        q2 = q[:, hd * D_KEY + D_SUBKEY:(hd + 1) * D_KEY]
        s1 = lax.dot_general(keys_ref[0, hd], q1, (((1,), (1,)), ((), ())), preferred_element_type=F32)
        s2 = lax.dot_general(keys_ref[1, hd], q2, (((1,), (1,)), ((), ())), preferred_element_type=F32)
        _topk_rows(s1, v1_s, i1_s, 0)
        _topk_rows(s2, v2_s, i2_s, 0)
        off = 0
        for i, n in enumerate(_CAND_ROWS):
            cand_s[off:off + n, :] = v1_s[i:i + 1, :] + v2_s[0:n, :]
            cidx_s[off:off + n, :] = i1_s[i:i + 1, :] * N_KEYS + i2_s[0:n, :]
            off += n
        _topk_rows(cand_s[...], top_s, exp_s, hd * PEER_TOPK, payload=cidx_s[...])
        ts = top_s[hd * PEER_TOPK:(hd + 1) * PEER_TOPK, :]
        e = jnp.exp(ts - jnp.max(ts, axis=0, keepdims=True))
        gate_s[hd * PEER_TOPK:(hd + 1) * PEER_TOPK, :] = e / jnp.sum(e, axis=0, keepdims=True)
    rows = exp_s[...] * SLAB
    idx_ref[...] = rows.T
    idxp_ref[...] = rows
    g_ref[...] = gate_s[...].T


def _route(h1, w):
    t = h1.shape[0]
    tb = 256 if t % 256 == 0 else 128
    return pl.pallas_call(
        functools.partial(_route_kernel, tb=tb),
        grid=(t // tb,),
        in_specs=[pl.BlockSpec((tb, D_MODEL), lambda i: (i, 0)),
                  _full((D_MODEL, PEER_HEADS * D_KEY)),
                  _full((2, PEER_HEADS, N_KEYS, D_SUBKEY))],
        out_specs=[pl.BlockSpec((tb, PAIRS), lambda i: (i, 0)), pl.BlockSpec((PAIRS, tb), lambda i: (i, 0)),
                   pl.BlockSpec((tb, PAIRS), lambda i: (i, 0))],
        out_shape=[jax.ShapeDtypeStruct((t, PAIRS), I32), jax.ShapeDtypeStruct((t // tb * PAIRS, tb), I32),
                   jax.ShapeDtypeStruct((t, PAIRS), F32)],
        scratch_shapes=[pltpu.VMEM((PEER_TOPK, tb), F32), pltpu.VMEM((PEER_TOPK, tb), I32),
                        pltpu.VMEM((PEER_TOPK, tb), F32), pltpu.VMEM((PEER_TOPK, tb), I32),
                        pltpu.VMEM((_N_CAND_PAD, tb), F32), pltpu.VMEM((_N_CAND_PAD, tb), I32),
                        pltpu.VMEM((PAIRS, tb), F32), pltpu.VMEM((PAIRS, tb), I32),
                        pltpu.VMEM((PAIRS, tb), F32)],
        compiler_params=_cparams(1),
        name="peer_route",
    )(h1, w["peer_w_query"], w["peer_sub_keys"])


def _rows_f32(slab):
    return pltpu.bitcast(slab, BF16).astype(F32)


def _gelu(x):
    return 0.5 * x * (1.0 + lax.erf(x * (1.0 / math.sqrt(2.0))))


def _peer_u_kernel(idx_ref, x_ref, g_ref, tab_ref, w_ref, prod_a, prod_b, act_s, *, tb):
    ones = jnp.ones((SUBLANES, LANES), BF16)
    ri = lax.broadcasted_iota(I32, (PAIR_ROWS, PAIRS), 0)
    ci = lax.broadcasted_iota(I32, (PAIR_ROWS, PAIRS), 1)
    group = (lax.shift_right_logical(ri, 2) == ci).astype(BF16)
    sub = lax.broadcasted_iota(I32, (SUBLANES, PAIR_ROWS), 0)

    def gather(t0, dst):
        for n in range(U_PHASE):
            t = t0 + n
            xt = x_ref[pl.ds(pl.multiple_of(t * SUBLANES, SUBLANES), SUBLANES), :]
            base = t * PAIRS
            for k in range(PAIRS):
                row = pl.multiple_of(idx_ref[base + k], SLAB)
                p = _rows_f32(tab_ref[pl.ds(row, SLAB), :]) * xt
                dst[n * PAIR_ROWS + k * SLAB:n * PAIR_ROWS + (k + 1) * SLAB, :] = p[0:SLAB] + p[SLAB:2 * SLAB]

    def reduce(src, t0):
        sums = lax.dot_general(ones, src[...].astype(BF16), (((1,), (1,)), ((), ())),
                               preferred_element_type=F32)
        for g in range(U_PHASE // SUBLANES):
            o = g * SUBLANES
            per_tok = sums[:, o * PAIR_ROWS:(o + 1) * PAIR_ROWS]
            for n in range(1, SUBLANES):
                per_tok = jnp.where(sub == n, sums[:, (o + n) * PAIR_ROWS:(o + n + 1) * PAIR_ROWS], per_tok)
            act_s[pl.ds(pl.multiple_of(t0 + o, SUBLANES), SUBLANES), :] = jnp.dot(
                per_tok.astype(BF16), group, preferred_element_type=F32)

    @pl.when(pl.program_id(0) == 0)
    def _():
        prod_b[...] = jnp.zeros((U_PHASE * PAIR_ROWS, LANES), F32)

    def body(j, carry):
        t0 = 2 * U_PHASE * j
        gather(t0, prod_a)
        reduce(prod_b, jnp.maximum(t0 - U_PHASE, 0))
        gather(t0 + U_PHASE, prod_b)
        reduce(prod_a, t0)
        return carry

    lax.fori_loop(0, tb // (2 * U_PHASE), body, 0)
    reduce(prod_b, tb - U_PHASE)
    w_ref[...] = g_ref[...] * _gelu(act_s[...])


def _peer_v_kernel(idx_ref, wt_ref, tab_ref, out_ref, wt_s, wb_a, wb_b, *, tb):
    wt_s[...] = wt_ref[...].T
    lane = lax.broadcasted_iota(I32, (PAIRS, tb), 1)

    def spread(t0, dst):
        for n in range(V_SPREAD):
            col = jnp.sum(jnp.where(lane == t0 + n, wt_s[...], 0.0), axis=1, keepdims=True)
            dst[n] = jnp.broadcast_to(col, (PAIRS, LANES))

    def phase(t0, wb, w0):
        toks = [t0 + n for n in range(V_PHASE)]

        def step(kc, accs):
            accs = list(accs)
            for kk in range(V_STEP):
                k = kc * V_STEP + kk
                offsets = idx_ref.at[k]
                for n in range(V_PHASE):
                    row = pl.multiple_of(offsets[toks[n]], SLAB)
                    accs[n] = accs[n] + wb[w0 + n, pl.ds(k, 1), :] * _rows_f32(tab_ref[pl.ds(row, SLAB), :])
            return tuple(accs)

        zeros = tuple(jnp.zeros((SUBLANES, LANES), F32) for _ in range(V_PHASE))
        accs = lax.fori_loop(0, PAIRS // V_STEP, step, zeros)
        for n in range(V_PHASE):
            out_ref[pl.ds(pl.multiple_of(toks[n] * SUBLANES, SUBLANES), SUBLANES), :] = accs[n]

    def phases(t0, wb):
        for h in range(V_SPREAD // V_PHASE):
            phase(t0 + h * V_PHASE, wb, h * V_PHASE)

    spread(0, wb_a)

    def body(j, carry):
        t0 = 2 * V_SPREAD * j
        spread(t0 + V_SPREAD, wb_b)
        phases(t0, wb_a)
        spread(jnp.minimum(t0 + 2 * V_SPREAD, tb - V_SPREAD), wb_a)
        phases(t0 + V_SPREAD, wb_b)
        return carry

    lax.fori_loop(0, tb // (2 * V_SPREAD), body, 0)


def _peer_tb(t):
    return next(c for c in (128, 64, 32) if t % c == 0)


def _table_spec(n_rows):
    return pl.BlockSpec((n_rows, LANES), lambda i: (0, 0), pipeline_mode=pl.Buffered(1))


def _peer_u(idx_flat, x8, gates, table):
    t = gates.shape[0]
    tb = _peer_tb(t)
    return pl.pallas_call(
        functools.partial(_peer_u_kernel, tb=tb),
        grid=(t // tb,),
        in_specs=[pl.BlockSpec((tb * PAIRS,), lambda i: (i,), memory_space=pltpu.SMEM),
                  pl.BlockSpec((tb * SUBLANES, LANES), lambda i: (i, 0)),
                  pl.BlockSpec((tb, PAIRS), lambda i: (i, 0)),
                  _table_spec(table.shape[0])],
        out_specs=pl.BlockSpec((tb, PAIRS), lambda i: (i, 0)),
        out_shape=jax.ShapeDtypeStruct((t, PAIRS), F32),
        scratch_shapes=[pltpu.VMEM((U_PHASE * PAIR_ROWS, LANES), F32),
                        pltpu.VMEM((U_PHASE * PAIR_ROWS, LANES), F32), pltpu.VMEM((tb, PAIRS), F32)],
        compiler_params=_cparams(1),
        name="peer_u",
    )(idx_flat, x8, gates, table)


def _peer_v(idx_pairs, wt, table):
    t = wt.shape[0]
    tb = _peer_tb(t)
    per_route = idx_pairs.shape[1] // tb
    return pl.pallas_call(
        functools.partial(_peer_v_kernel, tb=tb),
        grid=(t // tb,),
        in_specs=[pl.BlockSpec((PAIRS, tb), lambda i: (i // per_route, i % per_route), memory_space=pltpu.SMEM),
                  pl.BlockSpec((tb, PAIRS), lambda i: (i, 0)),
                  _table_spec(table.shape[0])],
        out_specs=pl.BlockSpec((tb * SUBLANES, LANES), lambda i: (i, 0)),
        out_shape=jax.ShapeDtypeStruct((t * SUBLANES, LANES), F32),
        scratch_shapes=[pltpu.VMEM((PAIRS, tb), F32), pltpu.VMEM((V_SPREAD, PAIRS, LANES), F32),
                        pltpu.VMEM((V_SPREAD, PAIRS, LANES), F32)],
        compiler_params=_cparams(1),
        name="peer_v",
    )(idx_pairs, wt, table)


def _ln2_kernel(h_ref, p_ref, g_ref, b_ref, o_ref):
    o_ref[...] = _ln(DEEPNORM_ALPHA * h_ref[...] + p_ref[...], g_ref[...], b_ref[...])


def _ln2(h1, peer, w):
    t = h1.shape[0]
    tb = next(c for c in (512, 256, 128) if t % c == 0)
    blk = pl.BlockSpec((tb, D_MODEL), lambda i: (i, 0))
    return pl.pallas_call(
        _ln2_kernel,
        grid=(t // tb,),
        in_specs=[blk, blk, _full((1, D_MODEL)), _full((1, D_MODEL))],
        out_specs=blk,
        out_shape=jax.ShapeDtypeStruct((t, D_MODEL), F32),
        compiler_params=_cparams(1),
        name="ln2",
    )(h1, peer, w["ln2_g"], w["ln2_b"])


def _pack_table(tab):
    e = tab.shape[0]
    bits = lax.bitcast_convert_type(tab.astype(BF16), jnp.uint16).astype(U32).reshape(e, SLAB, 2, LANES)
    return (bits[:, :, 0] | (bits[:, :, 1] << 16)).reshape(e * SLAB, LANES)


def _rot_cols(wr):
    half = QK_ROPE_DIM // 2
    return jnp.concatenate([-wr[..., half:], wr[..., :half]], axis=-1)


def _prep_weights(ln_in_g, ln_in_b, w_in, conv_w, conv_b, dt_bias_fwd, dt_bias_bwd, a_log_fwd, a_log_bwd,
                  d_skip, ssm_norm_g, q_norm_g, w_uq, kv_norm_g, w_ukv, attn_norm_g, w_out, ln1_g, ln1_b,
                  peer_w_query, peer_sub_keys, peer_u, peer_v, ln2_g, ln2_b):
    li = 0
    row = lambda v: v.reshape(1, -1).astype(F32)
    w = {}
    w["ln_in_g"], w["ln_in_b"] = row(ln_in_g), row(ln_in_b)
    wi = w_in[li]
    w_kr = wi[:, SPLIT_CKV:]
    zc = lambda n: jnp.zeros((D_MODEL, n), F32)
    tail_a = jnp.concatenate([wi[:, SPLIT_XBC:SPLIT_DT], w_kr, zc(LANES - KR_LANE - QK_ROPE_DIM)], axis=1)
    tail_b = jnp.concatenate([zc(KR_LANE), _rot_cols(w_kr), zc(LANES - KR_LANE - QK_ROPE_DIM)], axis=1)
    w["w_proj"] = jnp.concatenate([wi[:, :SPLIT_XBC], wi[:, SPLIT_DT:SPLIT_CKV], tail_a, tail_b],
                                  axis=1).astype(BF16)
    w["conv_w"], w["conv_b"] = conv_w[li].astype(F32), row(conv_b[li])
    hs = SSD_HEADS_PER_STEP
    ng = SSM_HEADS // hs
    per_dir = lambda f, b_: jnp.concatenate([f[li].reshape(ng, hs), b_[li].reshape(ng, hs)], axis=1)
    dtb, alog = per_dir(dt_bias_fwd, dt_bias_bwd), per_dir(a_log_fwd, a_log_bwd)
    dsk = per_dir(d_skip, d_skip)
    pcol = jnp.stack([dtb, alog, dsk], axis=1)
    w["ssd_pcol"] = jnp.pad(pcol, ((0, 0), (0, 0), (0, LANES - 2 * hs))).astype(F32)
    w["ssd_prow"] = jnp.stack([dtb, alog], axis=2).astype(F32)
    w["ssm_norm_g"] = row(ssm_norm_g[li])
    w["q_norm_g"], w["kv_norm_g"] = row(q_norm_g[li]), row(kv_norm_g[li])
    pad_h = lambda a, n: jnp.pad(a, ((0, 0), (0, 0), (0, HEAD_PAD - n)))
    uq = w_uq[li].reshape(Q_LORA_RANK, MLA_HEADS, QK_NOPE_DIM + QK_ROPE_DIM)
    uq_rot = jnp.concatenate([jnp.zeros_like(uq[..., :QK_NOPE_DIM]), _rot_cols(uq[..., QK_NOPE_DIM:])], axis=-1)
    w["w_uq"] = pad_h(uq, QK_NOPE_DIM + QK_ROPE_DIM).reshape(Q_LORA_RANK, D_HEADS_PAD).astype(BF16)
    w["w_uq_rot"] = pad_h(uq_rot, QK_NOPE_DIM + QK_ROPE_DIM).reshape(Q_LORA_RANK, D_HEADS_PAD).astype(BF16)
    ukv = w_ukv[li].reshape(KV_LORA_RANK, MLA_HEADS, QK_NOPE_DIM + V_HEAD_DIM)
    w["w_uk"] = pad_h(ukv[..., :QK_NOPE_DIM], QK_NOPE_DIM).reshape(KV_LORA_RANK, D_HEADS_PAD).astype(BF16)
    w["w_uv"] = pad_h(ukv[..., QK_NOPE_DIM:], V_HEAD_DIM).reshape(KV_LORA_RANK, D_HEADS_PAD).astype(BF16)
    src = KR_LANE + jnp.arange(QK_ROPE_DIM)
    e_kr = jnp.zeros((LANES, MLA_HEADS, HEAD_PAD), F32)
    e_kr = e_kr.at[src, :, QK_NOPE_DIM + jnp.arange(QK_ROPE_DIM)].set(1.0)
    w["e_kr"] = e_kr.reshape(LANES, D_HEADS_PAD).astype(BF16)
    w["attn_norm_g"] = pad_h(attn_norm_g[li].reshape(1, MLA_HEADS, V_HEAD_DIM), V_HEAD_DIM).reshape(1, D_HEADS_PAD)
    wo = w_out[li]
    w["w_out_ssm"] = wo[:D_SSM].astype(BF16)
    wo_attn = wo[D_SSM:].reshape(MLA_HEADS, V_HEAD_DIM, D_MODEL)
    w["w_out_attn"] = jnp.pad(wo_attn, ((0, 0), (0, HEAD_PAD - V_HEAD_DIM), (0, 0))).reshape(
        D_HEADS_PAD, D_MODEL).astype(BF16)
    w["ln1_g"], w["ln1_b"] = row(ln1_g[li]), row(ln1_b[li])
    w["peer_w_query"] = peer_w_query[li].astype(BF16)
    w["peer_sub_keys"] = peer_sub_keys[li].astype(BF16)
    w["u_table"] = _pack_table(peer_u[li])
    w["v_table"] = _pack_table(peer_v[li])
    w["ln2_g"], w["ln2_b"] = row(ln2_g[li]), row(ln2_b[li])
    return w


def _rope_tables(lp):
    half = QK_ROPE_DIM // 2
    pos = jnp.arange(lp, dtype=F32) - float(META_PAD)
    inv = ROPE_THETA ** (-jnp.arange(half, dtype=F32) / half)
    ang = pos[:, None] * inv[None, :]
    cos2 = jnp.tile(jnp.cos(ang), (1, 2))
    sin2 = jnp.tile(jnp.sin(ang), (1, 2))
    scale = 1.0 / math.sqrt(QK_NOPE_DIM + QK_ROPE_DIM)
    tail = HEAD_PAD - QK_NOPE_DIM - QK_ROPE_DIM
    cq = jnp.concatenate([jnp.ones((lp, QK_NOPE_DIM), F32), cos2, jnp.zeros((lp, tail), F32)], axis=1)
    sq = jnp.concatenate([jnp.zeros((lp, QK_NOPE_DIM), F32), sin2, jnp.zeros((lp, tail), F32)], axis=1)
    place = lambda a: jnp.pad(a, ((0, 0), (KR_LANE, LANES - KR_LANE - QK_ROPE_DIM)))
    key_bias = jnp.where(jnp.arange(lp) >= META_PAD, 0.0, -jnp.inf).astype(F32).reshape(1, lp)
    return {"cos_q": jnp.tile(cq, (1, MLA_HEADS)) * scale, "sin_q": jnp.tile(sq, (1, MLA_HEADS)) * scale,
            "cos_k": place(cos2), "sin_k": place(sin2), "key_bias": key_bias}


def _encode(x, meta_tokens, w):
    b, s, _ = x.shape
    lp = s + CHUNK
    lead = jnp.concatenate([jnp.zeros((META_PAD, D_MODEL), F32), meta_tokens.astype(F32)], axis=0)
    xpad = jnp.concatenate([jnp.broadcast_to(lead[None], (b, CHUNK, D_MODEL)), x], axis=1)
    tabs = _rope_tables(lp)
    z, xbc, tail, q, k, v = _inproj(xpad, tabs, w)
    hs = SSD_HEADS_PER_STEP
    ng = SSM_HEADS // hs
    dt = tail[..., :2 * SSM_HEADS].reshape(b, lp, 2, ng, hs)
    dt = jnp.transpose(dt, (0, 3, 1, 2, 4)).reshape(b, ng, lp, 2 * hs)
    dt_col = jnp.pad(dt, ((0, 0), (0, 0), (0, 0), (0, LANES - 2 * hs)))
    dt_row = jnp.swapaxes(dt, 2, 3)
    yg = _ssd(z, xbc, dt_col, dt_row, w)
    o = _attention(q, k, v, tabs["key_bias"])
    h1 = _outproj(x, yg, o, w).reshape(b * s, D_MODEL)
    idx, idx_pairs, gates = _route(h1, w)
    wt = _peer_u(idx.reshape(-1), h1.reshape(b * s * SUBLANES, LANES), gates, w["u_table"])
    peer = _peer_v(idx_pairs, wt, w["v_table"])
    return _ln2(h1, peer.reshape(b * s, D_MODEL), w).reshape(b, s, D_MODEL)


def kernel(x_prompt, x_sample, meta_tokens, ln_in_g, ln_in_b, w_in, conv_w, conv_b, dt_bias_fwd, dt_bias_bwd,
           a_log_fwd, a_log_bwd, d_skip, ssm_norm_g, q_norm_g, w_uq, kv_norm_g, w_ukv, attn_norm_g, w_out,
           ln1_g, ln1_b, peer_w_query, peer_sub_keys, peer_u, peer_v, ln2_g, ln2_b):
    w = _prep_weights(ln_in_g, ln_in_b, w_in, conv_w, conv_b, dt_bias_fwd, dt_bias_bwd, a_log_fwd, a_log_bwd,
                      d_skip, ssm_norm_g, q_norm_g, w_uq, kv_norm_g, w_ukv, attn_norm_g, w_out, ln1_g, ln1_b,
                      peer_w_query, peer_sub_keys, peer_u, peer_v, ln2_g, ln2_b)
    return (_encode(x_prompt, meta_tokens, w), _encode(x_sample, meta_tokens, w))
```

```python
import functools
import math

import jax
import jax.numpy as jnp
from jax import lax
from jax.experimental import pallas as pl
from jax.experimental.pallas import tpu as pltpu

F32 = jnp.float32
BF16 = jnp.bfloat16
I32 = jnp.int32
U32 = jnp.uint32

D_MODEL = 1024
N_META = 16
SSM_HEADS = 16
SSM_HEAD_DIM = 64
D_SSM = SSM_HEADS * SSM_HEAD_DIM
SSM_GROUPS = 2
D_STATE = 128
D_CONV = 5
D_CONV_CH = D_SSM + 2 * SSM_GROUPS * D_STATE
CHUNK = 128
META_PAD = CHUNK - N_META
MLA_HEADS = 8
QK_NOPE_DIM = 64
QK_ROPE_DIM = 32
V_HEAD_DIM = 64
Q_LORA_RANK = 384
KV_LORA_RANK = 256
D_ATTN = MLA_HEADS * V_HEAD_DIM
ROPE_THETA = 10000.0
SPLIT_Z = D_SSM
SPLIT_XBC = SPLIT_Z + D_CONV_CH
SPLIT_DT = SPLIT_XBC + 2 * SSM_HEADS
SPLIT_CQ = SPLIT_DT + Q_LORA_RANK
SPLIT_CKV = SPLIT_CQ + KV_LORA_RANK
PEER_HEADS = 8
N_KEYS = 128
PEER_TOPK = 16
D_KEY = 256
D_SUBKEY = D_KEY // 2
DEPTH = 1
DEEPNORM_ALPHA = (2.0 * DEPTH) ** 0.25
EPS = 1e-5

LANES = 128
SUBLANES = 8
HEAD_PAD = LANES
D_HEADS_PAD = MLA_HEADS * HEAD_PAD
VMEM_LIMIT = 56 * 1024 * 1024
SSD_HEADS_PER_STEP = 8
SSD_UNROLL = 2
ATTN_ROWS = 256
PAIRS = PEER_HEADS * PEER_TOPK
SLAB = 4
PAIR_ROWS = PAIRS * SLAB
U_PHASE = 2 * SUBLANES
V_PHASE = 4
V_STEP = 32
V_SPREAD = 2 * V_PHASE

OFF_Z = 0
OFF_XBC = D_SSM
OFF_CQ = OFF_XBC + D_CONV_CH
OFF_CKV = OFF_CQ + Q_LORA_RANK
OFF_TAIL_A = OFF_CKV + KV_LORA_RANK
OFF_TAIL_B = OFF_TAIL_A + LANES
D_PROJ = OFF_TAIL_B + LANES
KR_LANE = 2 * SSM_HEADS


def _cparams(n_axes):
    return pltpu.CompilerParams(dimension_semantics=("arbitrary",) * n_axes,
                                vmem_limit_bytes=VMEM_LIMIT)


def _full(shape):
    zeros = (0,) * len(shape)
    return pl.BlockSpec(shape, lambda *_: zeros)


def _ln(x, g, b):
    mu = jnp.mean(x, axis=-1, keepdims=True)
    xc = x - mu
    var = jnp.mean(xc * xc, axis=-1, keepdims=True)
    return xc * lax.rsqrt(var + EPS) * g + b


def _rms(x, g, n):
    ms = jnp.sum(x * x, axis=-1, keepdims=True) * (1.0 / n)
    return x * lax.rsqrt(ms + EPS) * g


def _mm(a, b):
    return jnp.dot(a.astype(BF16), b.astype(BF16), preferred_element_type=F32)


def _mm_nt(a, b):
    return lax.dot_general(a.astype(BF16), b.astype(BF16), (((1,), (1,)), ((), ())),
                           preferred_element_type=F32)


def _split3(a):
    a1 = a.astype(BF16)
    r1 = a - a1.astype(F32)
    a2 = r1.astype(BF16)
    a3 = (r1 - a2.astype(F32)).astype(BF16)
    return a1, a2, a3


def _tri_rows(tri, x):
    t = tri.astype(BF16)
    x1, x2, x3 = _split3(x)
    d = lambda p: jnp.dot(t, p, preferred_element_type=F32)
    return (d(x1) + d(x2)) + d(x3)


def _tri_cols(x, tri):
    t = tri.astype(BF16)
    x1, x2, x3 = _split3(x)
    d = lambda p: jnp.dot(p, t, preferred_element_type=F32)
    return (d(x1) + d(x2)) + d(x3)


def _sigmoid(x):
    return 1.0 / (1.0 + jnp.exp(-x))


def _softplus(x):
    return jnp.maximum(x, 0.0) + jnp.log(1.0 + jnp.exp(-jnp.abs(x)))


def _inproj_kernel(x_ref, cq_ref, sq_ref, ck_ref, sk_ref, lng_ref, lnb_ref, win_ref, qng_ref,
                   wuq_ref, wuqr_ref, kvng_ref, wuk_ref, wuv_ref, ekr_ref,
                   z_ref, xbc_ref, tail_ref, q_ref, k_ref, v_ref, *, tm):
    j = pl.program_id(1)
    h = _ln(x_ref[0], lng_ref[...], lnb_ref[...])
    proj = _mm(h, win_ref[...])
    row = j * tm + lax.broadcasted_iota(I32, (tm, 1), 0)
    valid = row >= META_PAD
    z_ref[0] = proj[:, OFF_Z:OFF_XBC]
    xbc_ref[0] = jnp.where(valid, proj[:, OFF_XBC:OFF_CQ], 0.0)
    tail = proj[:, OFF_TAIL_A:OFF_TAIL_B]
    tail_ref[0] = tail
    cqn = _rms(proj[:, OFF_CQ:OFF_CKV], qng_ref[...], Q_LORA_RANK)
    qf = _mm(cqn, wuq_ref[...]) * cq_ref[...] + _mm(cqn, wuqr_ref[...]) * sq_ref[...]
    ckvn = _rms(proj[:, OFF_CKV:OFF_TAIL_A], kvng_ref[...], KV_LORA_RANK)
    kt = tail * ck_ref[...] + proj[:, OFF_TAIL_B:D_PROJ] * sk_ref[...]
    kf = _mm(ckvn, wuk_ref[...]) + _mm(kt, ekr_ref[...])
    vf = _mm(ckvn, wuv_ref[...])
    for hd in range(MLA_HEADS):
        sl = slice(hd * HEAD_PAD, (hd + 1) * HEAD_PAD)
        q_ref[0, hd] = qf[:, sl].astype(BF16)
        k_ref[0, hd] = kf[:, sl].astype(BF16)
        v_ref[0, hd] = vf[:, sl].astype(BF16)


def _inproj(xpad, tabs, w):
    b, lp, _ = xpad.shape
    tm = next(t for t in (544, 384, 256, 128) if lp % t == 0)
    grid = (b, lp // tm)
    row_spec = lambda width: pl.BlockSpec((tm, width), lambda i, j: (j, 0))
    seq_spec = lambda width: pl.BlockSpec((1, tm, width), lambda i, j: (i, j, 0))
    head_spec = pl.BlockSpec((1, MLA_HEADS, tm, HEAD_PAD), lambda i, j: (i, 0, j, 0))
    head_shape = jax.ShapeDtypeStruct((b, MLA_HEADS, lp, HEAD_PAD), BF16)
    return pl.pallas_call(
        functools.partial(_inproj_kernel, tm=tm),
        grid=grid,
        in_specs=[seq_spec(D_MODEL), row_spec(D_HEADS_PAD), row_spec(D_HEADS_PAD), row_spec(LANES),
                  row_spec(LANES), _full((1, D_MODEL)), _full((1, D_MODEL)), _full((D_MODEL, D_PROJ)),
                  _full((1, Q_LORA_RANK)), _full((Q_LORA_RANK, D_HEADS_PAD)),
                  _full((Q_LORA_RANK, D_HEADS_PAD)), _full((1, KV_LORA_RANK)),
                  _full((KV_LORA_RANK, D_HEADS_PAD)), _full((KV_LORA_RANK, D_HEADS_PAD)),
                  _full((LANES, D_HEADS_PAD))],
        out_specs=[seq_spec(D_SSM), seq_spec(D_CONV_CH), seq_spec(LANES), head_spec, head_spec, head_spec],
        out_shape=[jax.ShapeDtypeStruct((b, lp, D_SSM), F32),
                   jax.ShapeDtypeStruct((b, lp, D_CONV_CH), F32),
                   jax.ShapeDtypeStruct((b, lp, LANES), F32),
                   head_shape, head_shape, head_shape],
        compiler_params=_cparams(2),
        name="inproj",
    )(xpad, tabs["cos_q"], tabs["sin_q"], tabs["cos_k"], tabs["sin_k"], w["ln_in_g"], w["ln_in_b"],
      w["w_proj"], w["q_norm_g"], w["w_uq"], w["w_uq_rot"], w["kv_norm_g"], w["w_uk"], w["w_uv"],
      w["e_kr"])


def _expand_heads(v, lane0, n_heads):
    lane = lax.broadcasted_iota(I32, (v.shape[0], LANES), 1)
    tiles = []
    for t in range(n_heads // 2):
        a = v[:, lane0 + 2 * t:lane0 + 2 * t + 1]
        b = v[:, lane0 + 2 * t + 1:lane0 + 2 * t + 2]
        tiles.append(jnp.where(lane < SSM_HEAD_DIM, a, b))
    return jnp.concatenate(tiles, axis=1)


def _conv_silu_chunk(win, w, bias, o, n):
    acc = bias + w[0:1] * win[o - 2:o - 2 + n]
    for k in range(1, D_CONV):
        acc = acc + w[k:k + 1] * win[o + k - 2:o + k - 2 + n]
    return acc * _sigmoid(acc)


def _conv_silu(src_ref, w_ref, b_ref, dst, nc):
    w = w_ref[...]
    bias = b_ref[...]
    lp = nc * CHUNK
    halo = SUBLANES
    width = src_ref.shape[-1]
    win = src_ref[0, 0:CHUNK + 2 * halo, :]
    dst[0:halo, :] = jnp.zeros((halo, width), F32)
    dst[halo:CHUNK, :] = _conv_silu_chunk(win, w, bias, halo, CHUNK - halo)

    def mid(c, carry):
        start = pl.multiple_of(c * CHUNK - halo, halo)
        win = src_ref[0, pl.ds(start, CHUNK + 2 * halo), :]
        dst[pl.ds(pl.multiple_of(c * CHUNK, CHUNK), CHUNK), :] = _conv_silu_chunk(win, w, bias, halo, CHUNK)
        return carry

    lax.fori_loop(1, nc - 1, mid, 0)
    win = jnp.concatenate([src_ref[0, lp - CHUNK - 2 * halo:lp, :], jnp.zeros((halo, width), F32)], axis=0)
    dst[lp - CHUNK:lp, :] = _conv_silu_chunk(win, w, bias, 2 * halo, CHUNK)


def _ssd_kernel(xs_ref, bm_ref, cm_ref, z_ref, cwx_ref, cbx_ref, cwb_ref, cbb_ref, cwc_ref, cbc_ref,
                dtc_ref, dtr_ref, pcol_ref, prow_ref, out_ref,
                xs_s, b_s, c_s, yf_s, yb_s, stf_s, stb_s, dtc_s, dtr_s, *, hs, nc):
    lp = nc * CHUNK
    cw = hs * SSM_HEAD_DIM
    pc = pcol_ref[0]
    pr = prow_ref[0]

    _conv_silu(xs_ref, cwx_ref, cbx_ref, xs_s, nc)
    _conv_silu(bm_ref, cwb_ref, cbb_ref, b_s, nc)
    _conv_silu(cm_ref, cwc_ref, cbc_ref, c_s, nc)

    rowi = lax.broadcasted_iota(I32, (lp, 1), 0)
    dtc_s[...] = jnp.where(rowi >= META_PAD, _softplus(dtc_ref[0, 0] + pc[0:1]), 0.0)
    lanei = lax.broadcasted_iota(I32, (1, lp), 1)
    dtr_s[...] = jnp.where(lanei >= META_PAD, _softplus(dtr_ref[0, 0] + pr[:, 0:1]), 0.0)
    a_col = -jnp.exp(pc[1:2])
    a_exp_f = _expand_heads(a_col, 0, hs)
    a_exp_b = _expand_heads(a_col, hs, hs)
    a_row = -jnp.exp(pr[:, 1:2])

    ri = lax.broadcasted_iota(I32, (CHUNK, CHUNK), 0)
    ci = lax.broadcasted_iota(I32, (CHUNK, CHUNK), 1)

    lane = lax.broadcasted_iota(I32, (CHUNK, LANES), 1)
    lower = ri >= ci
    upper = ri <= ci
    lower_f = lower.astype(F32)
    upper_f = upper.astype(F32)
    stf_s[...] = jnp.zeros((D_STATE, cw), F32)
    stb_s[...] = jnp.zeros((D_STATE, cw), F32)

    def scan_chunk(c, fwd, st_ref, y_ref):
        d0 = 0 if fwd else hs
        tri = lower if fwd else upper
        tmat = lower_f if fwd else upper_f
        umat = upper_f if fwd else lower_f
        r0 = pl.multiple_of(c * CHUNK, CHUNK)
        dte = _expand_heads(dtc_s[pl.ds(r0, CHUNK), :], d0, hs)
        cs_e = _tri_rows(tmat, dte * (a_exp_f if fwd else a_exp_b))
        a_r = dtr_s[:, pl.ds(r0, CHUNK)] * a_row
        cs_r = _tri_cols(a_r, umat)
        tot = cs_e[CHUNK - 1:CHUNK] if fwd else cs_e[0:1]
        x = xs_s[pl.ds(r0, CHUNK), :]
        bc = b_s[pl.ds(r0, CHUNK), :]
        cc = c_s[pl.ds(r0, CHUNK), :]
        xdt = x * dte
        cb = _mm_nt(cc, bc)
        st = st_ref[...]
        y = _mm(cc, st) * jnp.exp(cs_e)
        tiles = []
        for t in range(hs // 2):
            xt = xdt[:, t * LANES:(t + 1) * LANES]
            ys = []
            for hh in (2 * t, 2 * t + 1):
                seg = cs_e[:, hh * SSM_HEAD_DIM:hh * SSM_HEAD_DIM + 1] - cs_r[d0 + hh:d0 + hh + 1, :]
                lmat = jnp.exp(jnp.where(tri, seg, -jnp.inf))
                ys.append(_mm(cb * lmat, xt))
            tiles.append(jnp.where(lane < SSM_HEAD_DIM, ys[0], ys[1]))
        y_ref[pl.ds(r0, CHUNK), :] = y + jnp.concatenate(tiles, axis=1)
        xw = xdt * jnp.exp(tot - cs_e)
        st_ref[...] = st * jnp.exp(tot) + _mm(bc.T, xw)

    def step(i):
        scan_chunk(i, True, stf_s, yf_s)
        scan_chunk(nc - 1 - i, False, stb_s, yb_s)

    def body(j, carry):
        for u in range(SSD_UNROLL):
            step(j * SSD_UNROLL + u)
        return carry

    lax.fori_loop(0, nc // SSD_UNROLL, body, 0)
    for i in range(nc - nc % SSD_UNROLL, nc):
        step(i)

    d_skip = _expand_heads(pc[2:3], 0, hs)

    def fin(c, carry):
        r0 = pl.multiple_of(c * CHUNK, CHUNK)
        y = (yf_s[pl.ds(r0, CHUNK), :] + yb_s[pl.ds(r0, CHUNK), :]) + d_skip * xs_s[pl.ds(r0, CHUNK), :]
        zc = z_ref[0, pl.ds(r0, CHUNK), :]
        out_ref[0, pl.ds(pl.multiple_of(r0 - CHUNK, CHUNK), CHUNK), :] = y * (zc * _sigmoid(zc))
        return carry

    lax.fori_loop(1, nc, fin, 0)


def _ssd(z, xbc, dt_col, dt_row, w):
    b, lp, _ = z.shape
    nc = lp // CHUNK
    s = lp - CHUNK
    hs = SSD_HEADS_PER_STEP
    ng = SSM_HEADS // hs
    cw = hs * SSM_HEAD_DIM
    per_group = ng // SSM_GROUPS
    b_blk = D_SSM // D_STATE
    c_blk = b_blk + SSM_GROUPS
    x_map = lambda i, g: (i, 0, g)
    b_map = lambda i, g: (i, 0, b_blk + g // per_group)
    c_map = lambda i, g: (i, 0, c_blk + g // per_group)
    wx_map = lambda i, g: (0, g)
    wb_map = lambda i, g: (0, b_blk + g // per_group)
    wc_map = lambda i, g: (0, c_blk + g // per_group)
    return pl.pallas_call(
        functools.partial(_ssd_kernel, hs=hs, nc=nc),
        grid=(b, ng),
        in_specs=[pl.BlockSpec((1, lp, cw), x_map), pl.BlockSpec((1, lp, D_STATE), b_map),
                  pl.BlockSpec((1, lp, D_STATE), c_map), pl.BlockSpec((1, lp, cw), x_map),
                  pl.BlockSpec((D_CONV, cw), wx_map), pl.BlockSpec((1, cw), wx_map),
                  pl.BlockSpec((D_CONV, D_STATE), wb_map), pl.BlockSpec((1, D_STATE), wb_map),
                  pl.BlockSpec((D_CONV, D_STATE), wc_map), pl.BlockSpec((1, D_STATE), wc_map),
                  pl.BlockSpec((1, 1, lp, LANES), lambda i, g: (i, g, 0, 0)),
                  pl.BlockSpec((1, 1, 2 * hs, lp), lambda i, g: (i, g, 0, 0)),
                  pl.BlockSpec((1, 3, LANES), lambda i, g: (g, 0, 0)),
                  pl.BlockSpec((1, 2 * hs, 2), lambda i, g: (g, 0, 0))],
        out_specs=pl.BlockSpec((1, s, cw), x_map),
        out_shape=jax.ShapeDtypeStruct((b, s, D_SSM), F32),
        scratch_shapes=[pltpu.VMEM((lp, cw), F32), pltpu.VMEM((lp, D_STATE), F32),
                        pltpu.VMEM((lp, D_STATE), F32), pltpu.VMEM((lp, cw), F32), pltpu.VMEM((lp, cw), F32),
                        pltpu.VMEM((D_STATE, cw), F32), pltpu.VMEM((D_STATE, cw), F32),
                        pltpu.VMEM((lp, LANES), F32),
                        pltpu.VMEM((2 * hs, lp), F32)],
        compiler_params=_cparams(2),
        name="ssd",
    )(xbc, xbc, xbc, z, w["conv_w"], w["conv_b"], w["conv_w"], w["conv_b"], w["conv_w"], w["conv_b"],
      dt_col, dt_row, w["ssd_pcol"], w["ssd_prow"])


def _attn_kernel(q_ref, k_ref, v_ref, bias_ref, o_ref, *, tq):
    i = pl.program_id(2)
    rows = ATTN_ROWS
    for part in range(tq // rows):
        r0 = pl.multiple_of(CHUNK + i * tq + part * rows, CHUNK)
        q = q_ref[0, 0, pl.ds(r0, rows), :]
        s = lax.dot_general(q, k_ref[0, 0], (((1,), (1,)), ((), ())), preferred_element_type=F32)
        s = s + bias_ref[...]
        m = jnp.max(s, axis=-1, keepdims=True)
        p = jnp.exp(s - m)
        l = jnp.sum(p, axis=-1, keepdims=True)
        o = jnp.dot(p.astype(BF16), v_ref[0, 0], preferred_element_type=F32)
        o_ref[0, part * rows:(part + 1) * rows, :] = o / l


def _attention(q, k, v, key_bias):
    b, nh, lp, _ = q.shape
    s = lp - CHUNK
    tq = next(t for t in (2048, 1024, 512, 256) if s % t == 0)
    kv_spec = pl.BlockSpec((1, 1, lp, HEAD_PAD), lambda i, h, j: (i, h, 0, 0))
    return pl.pallas_call(
        functools.partial(_attn_kernel, tq=tq),
        grid=(b, nh, s // tq),
        in_specs=[kv_spec, kv_spec, kv_spec, _full((1, lp))],
        out_specs=pl.BlockSpec((1, tq, HEAD_PAD), lambda i, h, j: (i, j, h)),
        out_shape=jax.ShapeDtypeStruct((b, s, D_HEADS_PAD), F32),
        compiler_params=_cparams(3),
        name="attention",
    )(q, k, v, key_bias)


def _outproj_kernel(x_ref, y_ref, o_ref, lng_ref, lnb_ref, sg_ref, ag_ref, w1_ref, w2_ref,
                    g1_ref, b1_ref, h1_ref):
    h0 = _ln(x_ref[0], lng_ref[...], lnb_ref[...])
    ysn = _rms(y_ref[0], sg_ref[...], D_SSM)
    on = _rms(o_ref[0], ag_ref[...], D_ATTN)
    mix = _mm(ysn, w1_ref[...]) + _mm(on, w2_ref[...])
    h1_ref[0] = _ln(DEEPNORM_ALPHA * h0 + mix, g1_ref[...], b1_ref[...])


def _outproj(x, yg, o, w):
    b, s, _ = x.shape
    tb = next(t for t in (512, 256, 128) if s % t == 0)
    blk = lambda width: pl.BlockSpec((1, tb, width), lambda i, j: (i, j, 0))
    vec = lambda width: _full((1, width))
    return pl.pallas_call(
        _outproj_kernel,
        grid=(b, s // tb),
        in_specs=[blk(D_MODEL), blk(D_SSM), blk(D_HEADS_PAD), vec(D_MODEL), vec(D_MODEL), vec(D_SSM),
                  vec(D_HEADS_PAD), _full((D_SSM, D_MODEL)), _full((D_HEADS_PAD, D_MODEL)),
                  vec(D_MODEL), vec(D_MODEL)],
        out_specs=blk(D_MODEL),
        out_shape=jax.ShapeDtypeStruct((b, s, D_MODEL), F32),
        compiler_params=_cparams(2),
        name="outproj",
    )(x, yg, o, w["ln_in_g"], w["ln_in_b"], w["ssm_norm_g"], w["attn_norm_g"], w["w_out_ssm"],
      w["w_out_attn"], w["ln1_g"], w["ln1_b"])


def _topk_rows(s, vals_s, idx_s, row0, payload=None):
    n = s.shape[0]
    iota = lax.broadcasted_iota(I32, s.shape, 0)
    for r in range(PEER_TOPK):
        m = jnp.max(s, axis=0, keepdims=True)
        win = jnp.min(jnp.where(s == m, iota, n), axis=0, keepdims=True)
        hit = iota == win
        vals_s[row0 + r:row0 + r + 1, :] = m
        if payload is None:
            idx_s[row0 + r:row0 + r + 1, :] = win
        else:
            idx_s[row0 + r:row0 + r + 1, :] = jnp.sum(jnp.where(hit, payload, 0), axis=0, keepdims=True)
        s = jnp.where(hit, -jnp.inf, s)


_CAND_ROWS = [PEER_TOPK // (i + 1) for i in range(PEER_TOPK)]
_N_CAND = sum(_CAND_ROWS)
_N_CAND_PAD = -(-_N_CAND // SUBLANES) * SUBLANES


def _route_kernel(h_ref, wq_ref, keys_ref, idx_ref, idxp_ref, g_ref,
                  v1_s, i1_s, v2_s, i2_s, cand_s, cidx_s, top_s, exp_s, gate_s, *, tb):
    q = _mm(h_ref[...], wq_ref[...]).astype(BF16)
    cand_s[...] = jnp.full((_N_CAND_PAD, tb), -jnp.inf, F32)
    cidx_s[...] = jnp.zeros((_N_CAND_PAD, tb), I32)
    for hd in range(PEER_HEADS):
        q1 = q[:, hd * D_KEY:hd * D_KEY + D_SUBKEY]
        q2 = q[:, hd * D_KEY + D_SUBKEY:(hd + 1) * D_KEY]
        s1 = lax.dot_general(keys_ref[0, hd], q1, (((1,), (1,)), ((), ())), preferred_element_type=F32)
        s2 = lax.dot_general(keys_ref[1, hd], q2, (((1,), (1,)), ((), ())), preferred_element_type=F32)
        _topk_rows(s1, v1_s, i1_s, 0)
        _topk_rows(s2, v2_s, i2_s, 0)
        off = 0
        for i, n in enumerate(_CAND_ROWS):
            cand_s[off:off + n, :] = v1_s[i:i + 1, :] + v2_s[0:n, :]
            cidx_s[off:off + n, :] = i1_s[i:i + 1, :] * N_KEYS + i2_s[0:n, :]
            off += n
        _topk_rows(cand_s[...], top_s, exp_s, hd * PEER_TOPK, payload=cidx_s[...])
        ts = top_s[hd * PEER_TOPK:(hd + 1) * PEER_TOPK, :]
        e = jnp.exp(ts - jnp.max(ts, axis=0, keepdims=True))
        gate_s[hd * PEER_TOPK:(hd + 1) * PEER_TOPK, :] = e / jnp.sum(e, axis=0, keepdims=True)
    rows = exp_s[...] * SLAB
    idx_ref[...] = rows.T
    idxp_ref[...] = rows
    g_ref[...] = gate_s[...].T


def _route(h1, w):
    t = h1.shape[0]
    tb = 256 if t % 256 == 0 else 128
    return pl.pallas_call(
        functools.partial(_route_kernel, tb=tb),
        grid=(t // tb,),
        in_specs=[pl.BlockSpec((tb, D_MODEL), lambda i: (i, 0)),
                  _full((D_MODEL, PEER_HEADS * D_KEY)),
                  _full((2, PEER_HEADS, N_KEYS, D_SUBKEY))],
        out_specs=[pl.BlockSpec((tb, PAIRS), lambda i: (i, 0)), pl.BlockSpec((PAIRS, tb), lambda i: (i, 0)),
                   pl.BlockSpec((tb, PAIRS), lambda i: (i, 0))],
        out_shape=[jax.ShapeDtypeStruct((t, PAIRS), I32), jax.ShapeDtypeStruct((t // tb * PAIRS, tb), I32),
                   jax.ShapeDtypeStruct((t, PAIRS), F32)],
        scratch_shapes=[pltpu.VMEM((PEER_TOPK, tb), F32), pltpu.VMEM((PEER_TOPK, tb), I32),
                        pltpu.VMEM((PEER_TOPK, tb), F32), pltpu.VMEM((PEER_TOPK, tb), I32),
                        pltpu.VMEM((_N_CAND_PAD, tb), F32), pltpu.VMEM((_N_CAND_PAD, tb), I32),
                        pltpu.VMEM((PAIRS, tb), F32), pltpu.VMEM((PAIRS, tb), I32),
                        pltpu.VMEM((PAIRS, tb), F32)],
        compiler_params=_cparams(1),
        name="peer_route",
    )(h1, w["peer_w_query"], w["peer_sub_keys"])


def _rows_f32(slab):
    return pltpu.bitcast(slab, BF16).astype(F32)


def _gelu(x):
    return 0.5 * x * (1.0 + lax.erf(x * (1.0 / math.sqrt(2.0))))


def _peer_u_kernel(idx_ref, x_ref, g_ref, tab_ref, w_ref, prod_a, prod_b, act_s, *, tb):
    ones = jnp.ones((SUBLANES, LANES), BF16)
    ri = lax.broadcasted_iota(I32, (PAIR_ROWS, PAIRS), 0)
    ci = lax.broadcasted_iota(I32, (PAIR_ROWS, PAIRS), 1)
    group = (lax.shift_right_logical(ri, 2) == ci).astype(BF16)
    sub = lax.broadcasted_iota(I32, (SUBLANES, PAIR_ROWS), 0)

    def gather(t0, dst):
        for n in range(U_PHASE):
            t = t0 + n
            xt = x_ref[pl.ds(pl.multiple_of(t * SUBLANES, SUBLANES), SUBLANES), :]
            base = t * PAIRS
            for k in range(PAIRS):
                row = pl.multiple_of(idx_ref[base + k], SLAB)
                p = _rows_f32(tab_ref[pl.ds(row, SLAB), :]) * xt
                dst[n * PAIR_ROWS + k * SLAB:n * PAIR_ROWS + (k + 1) * SLAB, :] = p[0:SLAB] + p[SLAB:2 * SLAB]

    def reduce(src, t0):
        sums = lax.dot_general(ones, src[...].astype(BF16), (((1,), (1,)), ((), ())),
                               preferred_element_type=F32)
        for g in range(U_PHASE // SUBLANES):
            o = g * SUBLANES
            per_tok = sums[:, o * PAIR_ROWS:(o + 1) * PAIR_ROWS]
            for n in range(1, SUBLANES):
                per_tok = jnp.where(sub == n, sums[:, (o + n) * PAIR_ROWS:(o + n + 1) * PAIR_ROWS], per_tok)
            act_s[pl.ds(pl.multiple_of(t0 + o, SUBLANES), SUBLANES), :] = jnp.dot(
                per_tok.astype(BF16), group, preferred_element_type=F32)

    @pl.when(pl.program_id(0) == 0)
    def _():
        prod_b[...] = jnp.zeros((U_PHASE * PAIR_ROWS, LANES), F32)

    def body(j, carry):
        t0 = 2 * U_PHASE * j
        gather(t0, prod_a)
        reduce(prod_b, jnp.maximum(t0 - U_PHASE, 0))
        gather(t0 + U_PHASE, prod_b)
        reduce(prod_a, t0)
        return carry

    lax.fori_loop(0, tb // (2 * U_PHASE), body, 0)
    reduce(prod_b, tb - U_PHASE)
    w_ref[...] = g_ref[...] * _gelu(act_s[...])


def _peer_v_kernel(idx_ref, wt_ref, tab_ref, out_ref, wt_s, wb_a, wb_b, *, tb):
    wt_s[...] = wt_ref[...].T
    lane = lax.broadcasted_iota(I32, (PAIRS, tb), 1)

    def spread(t0, dst):
        for n in range(V_SPREAD):
            col = jnp.sum(jnp.where(lane == t0 + n, wt_s[...], 0.0), axis=1, keepdims=True)
            dst[n] = jnp.broadcast_to(col, (PAIRS, LANES))

    def phase(t0, wb, w0):
        toks = [t0 + n for n in range(V_PHASE)]

        def step(kc, accs):
            accs = list(accs)
            for kk in range(V_STEP):
                k = kc * V_STEP + kk
                offsets = idx_ref.at[k]
                for n in range(V_PHASE):
                    row = pl.multiple_of(offsets[toks[n]], SLAB)
                    accs[n] = accs[n] + wb[w0 + n, pl.ds(k, 1), :] * _rows_f32(tab_ref[pl.ds(row, SLAB), :])
            return tuple(accs)

        zeros = tuple(jnp.zeros((SUBLANES, LANES), F32) for _ in range(V_PHASE))
        accs = lax.fori_loop(0, PAIRS // V_STEP, step, zeros)
        for n in range(V_PHASE):
            out_ref[pl.ds(pl.multiple_of(toks[n] * SUBLANES, SUBLANES), SUBLANES), :] = accs[n]

    def phases(t0, wb):
        for h in range(V_SPREAD // V_PHASE):
            phase(t0 + h * V_PHASE, wb, h * V_PHASE)

    spread(0, wb_a)

    def body(j, carry):
        t0 = 2 * V_SPREAD * j
        spread(t0 + V_SPREAD, wb_b)
        phases(t0, wb_a)
        spread(jnp.minimum(t0 + 2 * V_SPREAD, tb - V_SPREAD), wb_a)
        phases(t0 + V_SPREAD, wb_b)
        return carry

    lax.fori_loop(0, tb // (2 * V_SPREAD), body, 0)


def _peer_tb(t):
    return next(c for c in (128, 64, 32) if t % c == 0)


def _table_spec(n_rows):
    return pl.BlockSpec((n_rows, LANES), lambda i: (0, 0), pipeline_mode=pl.Buffered(1))


def _peer_u(idx_flat, x8, gates, table):
    t = gates.shape[0]
    tb = _peer_tb(t)
    return pl.pallas_call(
        functools.partial(_peer_u_kernel, tb=tb),
        grid=(t // tb,),
        in_specs=[pl.BlockSpec((tb * PAIRS,), lambda i: (i,), memory_space=pltpu.SMEM),
                  pl.BlockSpec((tb * SUBLANES, LANES), lambda i: (i, 0)),
                  pl.BlockSpec((tb, PAIRS), lambda i: (i, 0)),
                  _table_spec(table.shape[0])],
        out_specs=pl.BlockSpec((tb, PAIRS), lambda i: (i, 0)),
        out_shape=jax.ShapeDtypeStruct((t, PAIRS), F32),
        scratch_shapes=[pltpu.VMEM((U_PHASE * PAIR_ROWS, LANES), F32),
                        pltpu.VMEM((U_PHASE * PAIR_ROWS, LANES), F32), pltpu.VMEM((tb, PAIRS), F32)],
        compiler_params=_cparams(1),
        name="peer_u",
    )(idx_flat, x8, gates, table)


def _peer_v(idx_pairs, wt, table):
    t = wt.shape[0]
    tb = _peer_tb(t)
    per_route = idx_pairs.shape[1] // tb
    return pl.pallas_call(
        functools.partial(_peer_v_kernel, tb=tb),
        grid=(t // tb,),
        in_specs=[pl.BlockSpec((PAIRS, tb), lambda i: (i // per_route, i % per_route), memory_space=pltpu.SMEM),
                  pl.BlockSpec((tb, PAIRS), lambda i: (i, 0)),
                  _table_spec(table.shape[0])],
        out_specs=pl.BlockSpec((tb * SUBLANES, LANES), lambda i: (i, 0)),
        out_shape=jax.ShapeDtypeStruct((t * SUBLANES, LANES), F32),
        scratch_shapes=[pltpu.VMEM((PAIRS, tb), F32), pltpu.VMEM((V_SPREAD, PAIRS, LANES), F32),
                        pltpu.VMEM((V_SPREAD, PAIRS, LANES), F32)],
        compiler_params=_cparams(1),
        name="peer_v",
    )(idx_pairs, wt, table)


def _ln2_kernel(h_ref, p_ref, g_ref, b_ref, o_ref):
    o_ref[...] = _ln(DEEPNORM_ALPHA * h_ref[...] + p_ref[...], g_ref[...], b_ref[...])


def _ln2(h1, peer, w):
    t = h1.shape[0]
    tb = next(c for c in (512, 256, 128) if t % c == 0)
    blk = pl.BlockSpec((tb, D_MODEL), lambda i: (i, 0))
    return pl.pallas_call(
        _ln2_kernel,
        grid=(t // tb,),
        in_specs=[blk, blk, _full((1, D_MODEL)), _full((1, D_MODEL))],
        out_specs=blk,
        out_shape=jax.ShapeDtypeStruct((t, D_MODEL), F32),
        compiler_params=_cparams(1),
        name="ln2",
    )(h1, peer, w["ln2_g"], w["ln2_b"])


def _pack_table(tab):
    e = tab.shape[0]
    bits = lax.bitcast_convert_type(tab.astype(BF16), jnp.uint16).astype(U32).reshape(e, SLAB, 2, LANES)
    return (bits[:, :, 0] | (bits[:, :, 1] << 16)).reshape(e * SLAB, LANES)


def _rot_cols(wr):
    half = QK_ROPE_DIM // 2
    return jnp.concatenate([-wr[..., half:], wr[..., :half]], axis=-1)


def _prep_weights(ln_in_g, ln_in_b, w_in, conv_w, conv_b, dt_bias_fwd, dt_bias_bwd, a_log_fwd, a_log_bwd,
                  d_skip, ssm_norm_g, q_norm_g, w_uq, kv_norm_g, w_ukv, attn_norm_g, w_out, ln1_g, ln1_b,
                  peer_w_query, peer_sub_keys, peer_u, peer_v, ln2_g, ln2_b):
    li = 0
    row = lambda v: v.reshape(1, -1).astype(F32)
    w = {}
    w["ln_in_g"], w["ln_in_b"] = row(ln_in_g), row(ln_in_b)
    wi = w_in[li]
    w_kr = wi[:, SPLIT_CKV:]
    zc = lambda n: jnp.zeros((D_MODEL, n), F32)
    tail_a = jnp.concatenate([wi[:, SPLIT_XBC:SPLIT_DT], w_kr, zc(LANES - KR_LANE - QK_ROPE_DIM)], axis=1)
    tail_b = jnp.concatenate([zc(KR_LANE), _rot_cols(w_kr), zc(LANES - KR_LANE - QK_ROPE_DIM)], axis=1)
    w["w_proj"] = jnp.concatenate([wi[:, :SPLIT_XBC], wi[:, SPLIT_DT:SPLIT_CKV], tail_a, tail_b],
                                  axis=1).astype(BF16)
    w["conv_w"], w["conv_b"] = conv_w[li].astype(F32), row(conv_b[li])
    hs = SSD_HEADS_PER_STEP
    ng = SSM_HEADS // hs
    per_dir = lambda f, b_: jnp.concatenate([f[li].reshape(ng, hs), b_[li].reshape(ng, hs)], axis=1)
    dtb, alog = per_dir(dt_bias_fwd, dt_bias_bwd), per_dir(a_log_fwd, a_log_bwd)
    dsk = per_dir(d_skip, d_skip)
    pcol = jnp.stack([dtb, alog, dsk], axis=1)
    w["ssd_pcol"] = jnp.pad(pcol, ((0, 0), (0, 0), (0, LANES - 2 * hs))).astype(F32)
    w["ssd_prow"] = jnp.stack([dtb, alog], axis=2).astype(F32)
    w["ssm_norm_g"] = row(ssm_norm_g[li])
    w["q_norm_g"], w["kv_norm_g"] = row(q_norm_g[li]), row(kv_norm_g[li])
    pad_h = lambda a, n: jnp.pad(a, ((0, 0), (0, 0), (0, HEAD_PAD - n)))
    uq = w_uq[li].reshape(Q_LORA_RANK, MLA_HEADS, QK_NOPE_DIM + QK_ROPE_DIM)
    uq_rot = jnp.concatenate([jnp.zeros_like(uq[..., :QK_NOPE_DIM]), _rot_cols(uq[..., QK_NOPE_DIM:])], axis=-1)
    w["w_uq"] = pad_h(uq, QK_NOPE_DIM + QK_ROPE_DIM).reshape(Q_LORA_RANK, D_HEADS_PAD).astype(BF16)
    w["w_uq_rot"] = pad_h(uq_rot, QK_NOPE_DIM + QK_ROPE_DIM).reshape(Q_LORA_RANK, D_HEADS_PAD).astype(BF16)
    ukv = w_ukv[li].reshape(KV_LORA_RANK, MLA_HEADS, QK_NOPE_DIM + V_HEAD_DIM)
    w["w_uk"] = pad_h(ukv[..., :QK_NOPE_DIM], QK_NOPE_DIM).reshape(KV_LORA_RANK, D_HEADS_PAD).astype(BF16)
    w["w_uv"] = pad_h(ukv[..., QK_NOPE_DIM:], V_HEAD_DIM).reshape(KV_LORA_RANK, D_HEADS_PAD).astype(BF16)
    src = KR_LANE + jnp.arange(QK_ROPE_DIM)
    e_kr = jnp.zeros((LANES, MLA_HEADS, HEAD_PAD), F32)
    e_kr = e_kr.at[src, :, QK_NOPE_DIM + jnp.arange(QK_ROPE_DIM)].set(1.0)
    w["e_kr"] = e_kr.reshape(LANES, D_HEADS_PAD).astype(BF16)
    w["attn_norm_g"] = pad_h(attn_norm_g[li].reshape(1, MLA_HEADS, V_HEAD_DIM), V_HEAD_DIM).reshape(1, D_HEADS_PAD)
    wo = w_out[li]
    w["w_out_ssm"] = wo[:D_SSM].astype(BF16)
    wo_attn = wo[D_SSM:].reshape(MLA_HEADS, V_HEAD_DIM, D_MODEL)
    w["w_out_attn"] = jnp.pad(wo_attn, ((0, 0), (0, HEAD_PAD - V_HEAD_DIM), (0, 0))).reshape(
        D_HEADS_PAD, D_MODEL).astype(BF16)
    w["ln1_g"], w["ln1_b"] = row(ln1_g[li]), row(ln1_b[li])
    w["peer_w_query"] = peer_w_query[li].astype(BF16)
    w["peer_sub_keys"] = peer_sub_keys[li].astype(BF16)
    w["u_table"] = _pack_table(peer_u[li])
    w["v_table"] = _pack_table(peer_v[li])
    w["ln2_g"], w["ln2_b"] = row(ln2_g[li]), row(ln2_b[li])
    return w


def _rope_tables(lp):
    half = QK_ROPE_DIM // 2
    pos = jnp.arange(lp, dtype=F32) - float(META_PAD)
    inv = ROPE_THETA ** (-jnp.arange(half, dtype=F32) / half)
    ang = pos[:, None] * inv[None, :]
    cos2 = jnp.tile(jnp.cos(ang), (1, 2))
    sin2 = jnp.tile(jnp.sin(ang), (1, 2))
    scale = 1.0 / math.sqrt(QK_NOPE_DIM + QK_ROPE_DIM)
    tail = HEAD_PAD - QK_NOPE_DIM - QK_ROPE_DIM
    cq = jnp.concatenate([jnp.ones((lp, QK_NOPE_DIM), F32), cos2, jnp.zeros((lp, tail), F32)], axis=1)
    sq = jnp.concatenate([jnp.zeros((lp, QK_NOPE_DIM), F32), sin2, jnp.zeros((lp, tail), F32)], axis=1)
    place = lambda a: jnp.pad(a, ((0, 0), (KR_LANE, LANES - KR_LANE - QK_ROPE_DIM)))
    key_bias = jnp.where(jnp.arange(lp) >= META_PAD, 0.0, -jnp.inf).astype(F32).reshape(1, lp)
    return {"cos_q": jnp.tile(cq, (1, MLA_HEADS)) * scale, "sin_q": jnp.tile(sq, (1, MLA_HEADS)) * scale,
            "cos_k": place(cos2), "sin_k": place(sin2), "key_bias": key_bias}


def _encode(x, meta_tokens, w):
    b, s, _ = x.shape
    lp = s + CHUNK
    lead = jnp.concatenate([jnp.zeros((META_PAD, D_MODEL), F32), meta_tokens.astype(F32)], axis=0)
    xpad = jnp.concatenate([jnp.broadcast_to(lead[None], (b, CHUNK, D_MODEL)), x], axis=1)
    tabs = _rope_tables(lp)
    z, xbc, tail, q, k, v = _inproj(xpad, tabs, w)
    hs = SSD_HEADS_PER_STEP
    ng = SSM_HEADS // hs
    dt = tail[..., :2 * SSM_HEADS].reshape(b, lp, 2, ng, hs)
    dt = jnp.transpose(dt, (0, 3, 1, 2, 4)).reshape(b, ng, lp, 2 * hs)
    dt_col = jnp.pad(dt, ((0, 0), (0, 0), (0, 0), (0, LANES - 2 * hs)))
    dt_row = jnp.swapaxes(dt, 2, 3)
    yg = _ssd(z, xbc, dt_col, dt_row, w)
    o = _attention(q, k, v, tabs["key_bias"])
    h1 = _outproj(x, yg, o, w).reshape(b * s, D_MODEL)
    idx, idx_pairs, gates = _route(h1, w)
    wt = _peer_u(idx.reshape(-1), h1.reshape(b * s * SUBLANES, LANES), gates, w["u_table"])
    peer = _peer_v(idx_pairs, wt, w["v_table"])
    return _ln2(h1, peer.reshape(b * s, D_MODEL), w).reshape(b, s, D_MODEL)


def kernel(x_prompt, x_sample, meta_tokens, ln_in_g, ln_in_b, w_in, conv_w, conv_b, dt_bias_fwd, dt_bias_bwd,
           a_log_fwd, a_log_bwd, d_skip, ssm_norm_g, q_norm_g, w_uq, kv_norm_g, w_ukv, attn_norm_g, w_out,
           ln1_g, ln1_b, peer_w_query, peer_sub_keys, peer_u, peer_v, ln2_g, ln2_b):
    w = _prep_weights(ln_in_g, ln_in_b, w_in, conv_w, conv_b, dt_bias_fwd, dt_bias_bwd, a_log_fwd, a_log_bwd,
                      d_skip, ssm_norm_g, q_norm_g, w_uq, kv_norm_g, w_ukv, attn_norm_g, w_out, ln1_g, ln1_b,
                      peer_w_query, peer_sub_keys, peer_u, peer_v, ln2_g, ln2_b)
    return (_encode(x_prompt, meta_tokens, w), _encode(x_sample, meta_tokens, w))
```
